```python
import math
import jax, jax.numpy as jnp
from jax import lax
import numpy as np

D_MODEL = 1024
BATCH = 4
SEQ = 8192
DEPTH = 2

N_MEM = 256
HEAD_DIM = 64
MEM_HEADS = 4
MEM_WIDTH = MEM_HEADS * HEAD_DIM
MAIN_WIDTH = D_MODEL - MEM_WIDTH

GLA_HEADS = 4
GLA_DV = MAIN_WIDTH // GLA_HEADS
GLA_DK = GLA_DV // 2
GLA_LOWRANK = 16
GLA_TAU = 16.0
GLA_CHUNK = 64

SWA_HEADS = MAIN_WIDTH // HEAD_DIM
SWA_KV_HEADS = 4
WINDOW = 128
ROPE_DIM = HEAD_DIM // 4
ROPE_THETA = 500000.0

D_FF = 2816
N_EXPERTS = 8
TOP_K = 2
D_FF_EXPERT = 3584

EPS = 1e-6
N_LAYERS_A = (DEPTH + 1) // 2
N_LAYERS_B = DEPTH // 2

W_IN_A = 2 * GLA_HEADS * GLA_DK + MAIN_WIDTH + GLA_LOWRANK + MAIN_WIDTH + MEM_WIDTH
W_IN_B = SWA_HEADS * HEAD_DIM + 2 * SWA_KV_HEADS * HEAD_DIM + MEM_WIDTH

kernel_name = "hybrid_gla_swa_sink_memxattn_moe"


def rms_norm(x, gain):
    xf = x.astype(jnp.float32)
    y = xf * lax.rsqrt(jnp.mean(xf * xf, axis=-1, keepdims=True) + EPS)
    return (y * gain.astype(jnp.float32)).astype(x.dtype)


def partial_rope(t, positions):
    half = ROPE_DIM // 2
    inv_freq = jnp.power(jnp.float32(ROPE_THETA), -jnp.arange(half, dtype=jnp.float32) / half)
    ang = positions.astype(jnp.float32)[:, :, None] * inv_freq
    cos = jnp.cos(ang)[:, :, None, :]
    sin = jnp.sin(ang)[:, :, None, :]
    tf = t.astype(jnp.float32)
    x1 = tf[..., :half]
    x2 = tf[..., half:ROPE_DIM]
    out = jnp.concatenate([x1 * cos - x2 * sin, x2 * cos + x1 * sin, tf[..., ROPE_DIM:]], axis=-1)
    return out.astype(t.dtype)


def gla_chunked(q, k, v, log_alpha):
    B, S, H, dk = q.shape
    dv = v.shape[-1]
    C = GLA_CHUNK
    N = S // C

    def to_chunks(t):
        return t.astype(jnp.float32).reshape(B, N, C, H, t.shape[-1]).transpose(1, 0, 3, 2, 4)

    qc, kc, vc, gc = to_chunks(q), to_chunks(k), to_chunks(v), to_chunks(log_alpha)
    b = jnp.cumsum(gc, axis=3)
    b_last = b[:, :, :, -1:, :]
    q_dec = qc * jnp.exp(b)
    k_dec = kc * jnp.exp(-b)
    k_to_end = kc * jnp.exp(b_last - b)

    causal = jnp.tril(jnp.ones((C, C), dtype=bool))
    attn = jnp.einsum('nbhtd,nbhsd->nbhts', q_dec, k_dec)
    attn = jnp.where(causal, attn, 0.0)
    o_intra = jnp.einsum('nbhts,nbhsv->nbhtv', attn, vc)

    chunk_kv = jnp.einsum('nbhsd,nbhsv->nbhdv', k_to_end, vc)
    chunk_decay = jnp.exp(b_last[:, :, :, 0, :])

    def step(state, inp):
        kv, dec = inp
        return dec[..., None] * state + kv, state

    state0 = jnp.zeros((B, H, dk, dv), jnp.float32)
    _, state_start = lax.scan(step, state0, (chunk_kv, chunk_decay))
    o_inter = jnp.einsum('nbhtd,nbhdv->nbhtv', q_dec, state_start)
    o = o_intra + o_inter
    return o.transpose(1, 0, 3, 2, 4).reshape(B, S, H, dv)


def sliding_window_gqa_sinks(q, k, v, sinks):
    B, S, H, D = q.shape
    Hkv = k.shape[2]
    G = H // Hkv
    W = WINDOW
    nb = S // W
    qb = q.reshape(B, nb, W, Hkv, G, D)

    def with_prev(t):
        cur = t.reshape(B, nb, W, Hkv, D)
        prev = jnp.concatenate([jnp.zeros_like(cur[:, :1]), cur[:, :-1]], axis=1)
        return jnp.concatenate([prev, cur], axis=2)

    kb, vb = with_prev(k), with_prev(v)
    s = jnp.einsum('bnqkgd,bnskd->bnkgqs', qb, kb).astype(jnp.float32) * (D ** -0.5)

    qi = jnp.arange(W)[:, None]
    kj = jnp.arange(2 * W)[None, :]
    dist = qi + W - kj
    band = (dist >= 0) & (dist < WINDOW)
    has_prev = (jnp.arange(nb)[:, None, None] > 0) | (kj >= W)[None]
    valid = (band[None] & has_prev)[None, :, None, None]

    sink = sinks.astype(jnp.float32).reshape(Hkv, G)[None, None, :, :, None]
    s = jnp.where(valid, s, -jnp.inf)
    m = jnp.maximum(jnp.max(s, axis=-1), sink)
    p = jnp.exp(s - m[..., None])
    denom = jnp.sum(p, axis=-1) + jnp.exp(sink - m)
    p = p / denom[..., None]
    out = jnp.einsum('bnkgqs,bnskd->bnqkgd', p.astype(vb.dtype), vb)
    return out.reshape(B, S, H * D)


def memory_cross_attention(qm, mem_n, w_mem_kv):
    B, S = qm.shape[:2]
    kv = mem_n @ w_mem_kv
    km = kv[..., :MEM_WIDTH].reshape(B, -1, MEM_HEADS, HEAD_DIM)
    vm = kv[..., MEM_WIDTH:].reshape(B, -1, MEM_HEADS, HEAD_DIM)
    s = jnp.einsum('bshd,bmhd->bhsm', qm, km).astype(jnp.float32) * (HEAD_DIM ** -0.5)
    p = jax.nn.softmax(s, axis=-1)
    out = jnp.einsum('bhsm,bmhd->bshd', p.astype(vm.dtype), vm)
    return out.reshape(B, S, MEM_WIDTH)


def swiglu(h, w_gate_up, w_down, d_ff):
    gu = h @ w_gate_up
    return (jax.nn.silu(gu[..., :d_ff]) * gu[..., d_ff:]) @ w_down


def moe_swiglu(h, router_w, exp_w_gate_up, exp_w_down):
    logits = (h @ router_w).astype(jnp.float32)
    top_v, top_i = lax.top_k(logits, TOP_K)
    gates = jax.nn.softmax(top_v, axis=-1)
    combine = jnp.sum(jax.nn.one_hot(top_i, N_EXPERTS, dtype=jnp.float32) * gates[..., None], axis=-2)
    y = jnp.zeros(h.shape, jnp.float32)
    for e in range(N_EXPERTS):
        y = y + combine[..., e:e + 1] * swiglu(h, exp_w_gate_up[e], exp_w_down[e], D_FF_EXPERT).astype(jnp.float32)
    return y.astype(h.dtype)


def setup_inputs(seed: int = 0) -> dict:
    key = jax.random.key(seed)
    ks = jax.random.split(key, 20)
    f32 = jnp.float32

    def w(k, shape, fan_in):
        return jax.random.normal(k, shape, f32) * (fan_in ** -0.5)

    def gain(k, shape):
        return 1.0 + 0.05 * jax.random.normal(k, shape, f32)

    x = jax.random.normal(ks[0], (BATCH, SEQ, D_MODEL), f32)
    mem = jax.random.normal(ks[1], (BATCH, N_MEM, D_MODEL), f32)
    offsets = jax.random.randint(ks[2], (BATCH, 1), 0, 4096, dtype=jnp.int32)
    positions = offsets + jnp.arange(SEQ, dtype=jnp.int32)[None, :]
    return {
        "x": x,
        "mem": mem,
        "positions": positions,
        "mix_norm": gain(ks[3], (DEPTH, D_MODEL)),
        "w_in_a": w(ks[4], (N_LAYERS_A, D_MODEL, W_IN_A), D_MODEL),
        "gla_gate_w": w(ks[5], (N_LAYERS_A, GLA_LOWRANK, GLA_HEADS * GLA_DK), GLA_LOWRANK),
        "gla_gate_b": 2.0 + 0.1 * jax.random.normal(ks[6], (N_LAYERS_A, GLA_HEADS * GLA_DK), f32),
        "gla_out_norm": gain(ks[7], (N_LAYERS_A, GLA_DV)),
        "w_in_b": w(ks[8], (N_LAYERS_B, D_MODEL, W_IN_B), D_MODEL),
        "swa_sinks": 0.5 * jax.random.normal(ks[9], (N_LAYERS_B, SWA_HEADS), f32),
        "mem_norm": gain(ks[10], (DEPTH, D_MODEL)),
        "w_mem_kv": w(ks[11], (DEPTH, D_MODEL, 2 * MEM_WIDTH), D_MODEL),
        "w_out": w(ks[12], (DEPTH, D_MODEL, D_MODEL), D_MODEL),
        "ffn_norm": gain(ks[13], (DEPTH, D_MODEL)),
        "ffn_w_gate_up": w(ks[14], (N_LAYERS_A, D_MODEL, 2 * D_FF), D_MODEL),
        "ffn_w_down": w(ks[15], (N_LAYERS_A, D_FF, D_MODEL), D_FF),
        "router_w": w(ks[16], (N_LAYERS_B, D_MODEL, N_EXPERTS), D_MODEL),
        "exp_w_gate_up": w(ks[17], (N_LAYERS_B, N_EXPERTS, D_MODEL, 2 * D_FF_EXPERT), D_MODEL),
        "exp_w_down": w(ks[18], (N_LAYERS_B, N_EXPERTS, D_FF_EXPERT, D_MODEL), D_FF_EXPERT),
        "final_norm": gain(ks[19], (D_MODEL,)),
    }


def reference(x, mem, positions, mix_norm, w_in_a, gla_gate_w, gla_gate_b, gla_out_norm,
              w_in_b, swa_sinks, mem_norm, w_mem_kv, w_out, ffn_norm, ffn_w_gate_up,
              ffn_w_down, router_w, exp_w_gate_up, exp_w_down, final_norm):
    B, S, _ = x.shape
    for i in range(DEPTH):
        j = i // 2
        h = rms_norm(x, mix_norm[i])
        mem_n = rms_norm(mem, mem_norm[i])
        if i % 2 == 0:
            z = h @ w_in_a[j]
            dqk = GLA_HEADS * GLA_DK
            o0 = 0
            q = z[..., o0:o0 + dqk]; o0 += dqk
            k = z[..., o0:o0 + dqk]; o0 += dqk
            v = z[..., o0:o0 + MAIN_WIDTH]; o0 += MAIN_WIDTH
            g_lr = z[..., o0:o0 + GLA_LOWRANK]; o0 += GLA_LOWRANK
            r = z[..., o0:o0 + MAIN_WIDTH]; o0 += MAIN_WIDTH
            qm = z[..., o0:o0 + MEM_WIDTH]
            gate_logits = (g_lr @ gla_gate_w[j] + gla_gate_b[j]).astype(jnp.float32)
            log_alpha = (jax.nn.log_sigmoid(gate_logits) / GLA_TAU).reshape(B, S, GLA_HEADS, GLA_DK)
            q = q.reshape(B, S, GLA_HEADS, GLA_DK) * (GLA_DK ** -0.5)
            k = k.reshape(B, S, GLA_HEADS, GLA_DK)
            v = v.reshape(B, S, GLA_HEADS, GLA_DV)
            o = gla_chunked(q, k, v, log_alpha)
            o = rms_norm(o, gla_out_norm[j]).reshape(B, S, MAIN_WIDTH)
            main = (o * jax.nn.silu(r.astype(jnp.float32))).astype(x.dtype)
        else:
            z = h @ w_in_b[j]
            dq = SWA_HEADS * HEAD_DIM
            dkv = SWA_KV_HEADS * HEAD_DIM
            q = z[..., :dq].reshape(B, S, SWA_HEADS, HEAD_DIM)
            k = z[..., dq:dq + dkv].reshape(B, S, SWA_KV_HEADS, HEAD_DIM)
            v = z[..., dq + dkv:dq + 2 * dkv].reshape(B, S, SWA_KV_HEADS, HEAD_DIM)
            qm = z[..., dq + 2 * dkv:]
            q = partial_rope(q, positions)
            k = partial_rope(k, positions)
            main = sliding_window_gqa_sinks(q, k, v, swa_sinks[j]).astype(x.dtype)
        mem_out = memory_cross_attention(qm.reshape(B, S, MEM_HEADS, HEAD_DIM), mem_n, w_mem_kv[i])
        mixed = jnp.concatenate([main, mem_out.astype(x.dtype)], axis=-1) @ w_out[i]
        x = x + mixed
        h = rms_norm(x, ffn_norm[i])
        if i % 2 == 0:
            x = x + swiglu(h, ffn_w_gate_up[j], ffn_w_down[j], D_FF)
        else:
            x = x + moe_swiglu(h, router_w[j], exp_w_gate_up[j], exp_w_down[j])
    return rms_norm(x, final_norm)
```

```python
import functools

import jax
import jax.numpy as jnp
from jax import lax
from jax.experimental import pallas as pl
from jax.experimental.pallas import tpu as pltpu

F32 = jnp.float32
BF16 = jnp.bfloat16

HEAD_DIM = 64
MEM_HEADS = 4
MEM_WIDTH = MEM_HEADS * HEAD_DIM
GLA_HEADS = 4
GLA_DK = 96
GLA_DV = 192
GLA_LOWRANK = 16
GLA_TAU = 16.0
GLA_CHUNK = 64
SWA_HEADS = 12
SWA_KV_HEADS = 4
WINDOW = 128
ROPE_DIM = 16
ROPE_THETA = 500000.0
N_EXPERTS = 8
EPS = 1e-6

LANES = 128
GLA_DK_PAD = 128
GLA_DV_PAD = 256
VMEM_LIMIT = 56 * 1024 * 1024

NT_DIMS = (((1,), (1,)), ((), ()))
TN_DIMS = (((0,), (0,)), ((), ()))


def _params(sem):
    return pltpu.CompilerParams(dimension_semantics=sem, vmem_limit_bytes=VMEM_LIMIT)


def _rms(xf, gain):
    return xf * lax.rsqrt(jnp.mean(xf * xf, axis=-1, keepdims=True) + EPS) * gain


def _dot(a, b):
    return jnp.dot(a, b, preferred_element_type=F32)


def _silu(x):
    return x * jax.nn.sigmoid(x)


def _memkv_kernel(mem_ref, gain_ref, w_ref, km_ref, vm_ref):
    mem_n = _rms(mem_ref[0], gain_ref[0]).astype(BF16)
    kv = _dot(mem_n, w_ref[0].astype(BF16))
    k = kv[:, :MEM_WIDTH] * (HEAD_DIM ** -0.5)
    v = kv[:, MEM_WIDTH:]
    head = lax.broadcasted_iota(jnp.int32, k.shape, 1) // HEAD_DIM
    for h in range(MEM_HEADS):
        km_ref[0, 0, h] = jnp.where(head == h, k, 0.0).astype(BF16)
        vm_ref[0, 0, h] = jnp.where(head == h, v, 0.0).astype(BF16)


def _memkv(mem, mem_norm, w_mem_kv):
    depth = mem_norm.shape[0]
    b, n_mem, d = mem.shape
    out = jax.ShapeDtypeStruct((depth, b, MEM_HEADS, n_mem, MEM_WIDTH), BF16)
    blk = pl.BlockSpec((1, 1, MEM_HEADS, n_mem, MEM_WIDTH), lambda i, bb: (i, bb, 0, 0, 0))
    return pl.pallas_call(
        _memkv_kernel,
        grid=(depth, b),
        in_specs=[
            pl.BlockSpec((1, n_mem, d), lambda i, bb: (bb, 0, 0)),
            pl.BlockSpec((1, 1, d), lambda i, bb: (i, 0, 0)),
            pl.BlockSpec((1, d, 2 * MEM_WIDTH), lambda i, bb: (i, 0, 0)),
        ],
        out_specs=[blk, blk],
        out_shape=[out, out],
        compiler_params=_params(("arbitrary", "arbitrary")),
        name="mem_kv",
    )(mem, mem_norm.reshape(depth, 1, d), w_mem_kv)


def _mem_attention(qm, km_ref, vm_ref):
    acc = jnp.zeros((qm.shape[0], MEM_WIDTH), F32)
    for h in range(MEM_HEADS):
        s = lax.dot_general(qm, km_ref[h], NT_DIMS, preferred_element_type=F32)
        p = jnp.exp(s - jnp.max(s, axis=-1, keepdims=True))
        p = p / jnp.sum(p, axis=-1, keepdims=True)
        acc = acc + _dot(p.astype(BF16), vm_ref[h])
    return acc


def _proj_kernel(x_ref, gain_ref, w_ref, z_ref, *, col_chunk):
    h = _rms(x_ref[...], gain_ref[...]).astype(BF16)
    n = w_ref.shape[1]
    for c0 in range(0, n, col_chunk):
        z_ref[:, c0:c0 + col_chunk] = _dot(h, w_ref[:, c0:c0 + col_chunk]).astype(BF16)


def _proj_rope_kernel(x_ref, gain_ref, w_ref, pos_ref, invf_ref, z_ref, *, rope_cols):
    h = _rms(x_ref[...], gain_ref[...]).astype(BF16)
    ang = pos_ref[...] * invf_ref[...]
    lane = lax.broadcasted_iota(jnp.int32, ang.shape, 1) % HEAD_DIM
    half = ROPE_DIM // 2
    cosf = jnp.where(lane < ROPE_DIM, jnp.cos(ang), 1.0)
    sn = jnp.sin(ang)
    sinf = jnp.where(lane < half, -sn, jnp.where(lane < ROPE_DIM, sn, 0.0))
    first = lane < half
    n = w_ref.shape[1]
    for c0 in range(0, n, LANES):
        zc = _dot(h, w_ref[:, c0:c0 + LANES])
        if c0 < rope_cols:
            partner = jnp.where(first, pltpu.roll(zc, LANES - half, 1), pltpu.roll(zc, half, 1))
            zc = zc * cosf + partner * sinf
        z_ref[:, c0:c0 + LANES] = zc.astype(BF16)


def _proj(x2, gain, w, tm, rope=None):
    t, d = x2.shape
    n = w.shape[1]
    in_specs = [
        pl.BlockSpec((tm, d), lambda i: (i, 0)),
        pl.BlockSpec((1, d), lambda i: (0, 0)),
        pl.BlockSpec((d, n), lambda i: (0, 0)),
    ]
    args = [x2, gain.reshape(1, d), w]
    if rope is None:
        kern = functools.partial(_proj_kernel, col_chunk=n // 3)
    else:
        pos, invf, rope_cols = rope
        kern = functools.partial(_proj_rope_kernel, rope_cols=rope_cols)
        in_specs += [pl.BlockSpec((tm, 1), lambda i: (i, 0)), pl.BlockSpec((1, LANES), lambda i: (0, 0))]
        args += [pos, invf]
    return pl.pallas_call(
        kern,
        grid=(t // tm,),
        in_specs=in_specs,
        out_specs=pl.BlockSpec((tm, n), lambda i: (i, 0)),
        out_shape=jax.ShapeDtypeStruct((t, n), BF16),
        compiler_params=_params(("parallel",)),
        name="in_proj" if rope is None else "in_proj_rope",
    )(*args)


def _gla_mix_kernel(q_ref, k_ref, v_ref, r_ref, qm_ref, gl_ref, x_ref, gw_ref, gb_ref, on_ref,
                    km_ref, vm_ref, wo_ref, out_ref, la_s, main_s, st_s):
    @pl.when(pl.program_id(1) == 0)
    def _():
        st_s[...] = jnp.zeros_like(st_s)

    tq = q_ref.shape[0]
    c_len = GLA_CHUNK
    logits = _dot(gl_ref[...], gw_ref[...]) + gb_ref[...]
    log_sig = jnp.minimum(logits, 0.0) - jnp.log(1.0 + jnp.exp(-jnp.abs(logits)))
    lane = lax.broadcasted_iota(jnp.int32, logits.shape, 1) % GLA_DK_PAD
    la_s[...] = jnp.where(lane < GLA_DK, log_sig * (1.0 / GLA_TAU), 0.0)

    row = lax.broadcasted_iota(jnp.int32, (c_len, c_len), 0)
    col = lax.broadcasted_iota(jnp.int32, (c_len, c_len), 1)
    causal = col <= row
    tri = causal.astype(F32)
    scale = GLA_DK ** -0.5

    def chunk(c, carry):
        r0 = pl.multiple_of(c * c_len, c_len)
        rows = pl.ds(r0, c_len)
        for h in range(GLA_HEADS):
            kcols = slice(h * GLA_DK_PAD, (h + 1) * GLA_DK_PAD)
            vcols = slice(h * GLA_DV_PAD, (h + 1) * GLA_DV_PAD)
            la = la_s[rows, kcols]
            b = jnp.dot(tri, la, precision=lax.Precision.HIGHEST, preferred_element_type=F32)
            b_last = b[c_len - 1:c_len, :]
            qf = q_ref[rows, kcols].astype(F32)
            kf = k_ref[rows, kcols].astype(F32)
            qd = (qf * scale * jnp.exp(b)).astype(BF16)
            kd = (kf * jnp.exp(-b)).astype(BF16)
            ke = (kf * jnp.exp(b_last - b)).astype(BF16)
            vc = v_ref[rows, vcols]
            attn = lax.dot_general(qd, kd, NT_DIMS, preferred_element_type=F32)
            attn = jnp.where(causal, attn, 0.0).astype(BF16)
            st = st_s[h]
            o = _dot(attn, vc) + lax.dot_general(qd, st.astype(BF16), NT_DIMS, preferred_element_type=F32)
            st_s[h] = st * jnp.exp(b_last) + lax.dot_general(vc, ke, TN_DIMS, preferred_element_type=F32)
            ms = jnp.sum(o * o, axis=-1, keepdims=True) * (1.0 / GLA_DV)
            o_n = o * lax.rsqrt(ms + EPS) * on_ref[...]
            main_s[rows, vcols] = (o_n * _silu(r_ref[rows, vcols].astype(F32))).astype(BF16)
        return carry

    lax.fori_loop(0, tq // c_len, chunk, 0)

    mem_out = _mem_attention(qm_ref[...], km_ref, vm_ref)
    n_main = GLA_HEADS * GLA_DV_PAD
    mixed = _dot(main_s[...], wo_ref[0:n_main, :]) + _dot(mem_out.astype(BF16), wo_ref[n_main:, :])
    out_ref[...] = x_ref[...] + mixed


def _gla_mix(z, x2, gate_w, gate_b, out_gain, km, vm, wo, b, s, tq):
    t, d = x2.shape
    nj = s // tq
    wq = GLA_HEADS * GLA_DK_PAD
    wv = GLA_HEADS * GLA_DV_PAD
    row = lambda bb, j: bb * nj + j
    qm_blk = (2 * wq + 2 * wv) // MEM_WIDTH
    gl_blk = (2 * wq + 2 * wv + MEM_WIDTH) // LANES
    n_mem = km.shape[2]
    in_specs = [
        pl.BlockSpec((tq, wq), lambda bb, j: (row(bb, j), 0)),
        pl.BlockSpec((tq, wq), lambda bb, j: (row(bb, j), 1)),
        pl.BlockSpec((tq, wv), lambda bb, j: (row(bb, j), 1)),
        pl.BlockSpec((tq, wv), lambda bb, j: (row(bb, j), 2)),
        pl.BlockSpec((tq, MEM_WIDTH), lambda bb, j: (row(bb, j), qm_blk)),
        pl.BlockSpec((tq, LANES), lambda bb, j: (row(bb, j), gl_blk)),
        pl.BlockSpec((tq, d), lambda bb, j: (row(bb, j), 0)),
        pl.BlockSpec(gate_w.shape, lambda bb, j: (0, 0)),
        pl.BlockSpec(gate_b.shape, lambda bb, j: (0, 0)),
        pl.BlockSpec(out_gain.shape, lambda bb, j: (0, 0)),
        pl.BlockSpec((None, MEM_HEADS, n_mem, MEM_WIDTH), lambda bb, j: (bb, 0, 0, 0)),
        pl.BlockSpec((None, MEM_HEADS, n_mem, MEM_WIDTH), lambda bb, j: (bb, 0, 0, 0)),
        pl.BlockSpec(wo.shape, lambda bb, j: (0, 0)),
    ]
    return pl.pallas_call(
        _gla_mix_kernel,
        grid=(b, nj),
        in_specs=in_specs,
        out_specs=pl.BlockSpec((tq, d), lambda bb, j: (row(bb, j), 0)),
        out_shape=jax.ShapeDtypeStruct((t, d), F32),
        scratch_shapes=[
            pltpu.VMEM((tq, wq), F32),
            pltpu.VMEM((tq, wv), BF16),
            pltpu.VMEM((GLA_HEADS, GLA_DV_PAD, GLA_DK_PAD), F32),
        ],
        compiler_params=_params(("arbitrary", "arbitrary")),
        name="gla_mix",
    )(z, z, z, z, z, z, x2, gate_w, gate_b, out_gain, km, vm, wo)


def _swa_mix_kernel(sink_ref, q_ref, k_ref, v_ref, kp_ref, vp_ref, qm_ref, x_ref, km_ref, vm_ref,
                    wo_ref, out_ref, main_s):
    has_prev = pl.program_id(1) > 0
    tq = q_ref.shape[0]
    w = WINDOW
    qi = lax.broadcasted_iota(jnp.int32, (w, w), 0)
    kj = lax.broadcasted_iota(jnp.int32, (w, w), 1)
    hi_valid = kj <= qi
    lo_band = kj > qi
    group = SWA_HEADS // SWA_KV_HEADS
    neg_inf = -jnp.inf
    for sub in range(tq // w):
        r0 = sub * w
        if sub == 0:
            k_lo, v_lo = kp_ref[...], vp_ref[...]
            lo_valid = jnp.logical_and(lo_band, has_prev)
        else:
            k_lo, v_lo = k_ref[r0 - w:r0, :], v_ref[r0 - w:r0, :]
            lo_valid = lo_band
        k_hi, v_hi = k_ref[r0:r0 + w, :], v_ref[r0:r0 + w, :]
        for h in range(SWA_HEADS):
            g = h // group
            gc = slice(g * HEAD_DIM, (g + 1) * HEAD_DIM)
            hc = slice(h * HEAD_DIM, (h + 1) * HEAD_DIM)
            qh = q_ref[r0:r0 + w, hc]
            s_lo = lax.dot_general(qh, k_lo[:, gc], NT_DIMS, preferred_element_type=F32)
            s_hi = lax.dot_general(qh, k_hi[:, gc], NT_DIMS, preferred_element_type=F32)
            s_lo = jnp.where(lo_valid, s_lo, neg_inf)
            s_hi = jnp.where(hi_valid, s_hi, neg_inf)
            sink = sink_ref[h]
            m = jnp.maximum(jnp.max(s_lo, axis=-1, keepdims=True), jnp.max(s_hi, axis=-1, keepdims=True))
            m = jnp.maximum(m, sink)
            p_lo = jnp.exp(s_lo - m)
            p_hi = jnp.exp(s_hi - m)
            den = (jnp.sum(p_lo, axis=-1, keepdims=True) + jnp.sum(p_hi, axis=-1, keepdims=True)
                   + jnp.exp(sink - m))
            o = _dot(p_lo.astype(BF16), v_lo[:, gc]) + _dot(p_hi.astype(BF16), v_hi[:, gc])
            main_s[r0:r0 + w, hc] = (o / den).astype(BF16)

    mem_out = _mem_attention(qm_ref[...], km_ref, vm_ref)
    n_main = SWA_HEADS * HEAD_DIM
    mixed = _dot(main_s[...], wo_ref[0:n_main, :]) + _dot(mem_out.astype(BF16), wo_ref[n_main:, :])
    out_ref[...] = x_ref[...] + mixed


def _swa_mix(z, x2, sinks, km, vm, wo, b, s, tq):
    t, d = x2.shape
    nj = s // tq
    wq = SWA_HEADS * HEAD_DIM
    wkv = SWA_KV_HEADS * HEAD_DIM
    row = lambda bb, j: bb * nj + j
    per = tq // WINDOW
    prev = lambda bb, j: jnp.maximum(row(bb, j) * per - 1, 0)
    k_blk = wq // wkv
    n_mem = km.shape[2]
    in_specs = [
        pl.BlockSpec(memory_space=pltpu.SMEM),
        pl.BlockSpec((tq, wq), lambda bb, j: (row(bb, j), 0)),
        pl.BlockSpec((tq, wkv), lambda bb, j: (row(bb, j), k_blk)),
        pl.BlockSpec((tq, wkv), lambda bb, j: (row(bb, j), k_blk + 1)),
        pl.BlockSpec((WINDOW, wkv), lambda bb, j: (prev(bb, j), k_blk)),
        pl.BlockSpec((WINDOW, wkv), lambda bb, j: (prev(bb, j), k_blk + 1)),
        pl.BlockSpec((tq, MEM_WIDTH), lambda bb, j: (row(bb, j), k_blk + 2)),
        pl.BlockSpec((tq, d), lambda bb, j: (row(bb, j), 0)),
        pl.BlockSpec((None, MEM_HEADS, n_mem, MEM_WIDTH), lambda bb, j: (bb, 0, 0, 0)),
        pl.BlockSpec((None, MEM_HEADS, n_mem, MEM_WIDTH), lambda bb, j: (bb, 0, 0, 0)),
        pl.BlockSpec(wo.shape, lambda bb, j: (0, 0)),
    ]
    return pl.pallas_call(
        _swa_mix_kernel,
        grid=(b, nj),
        in_specs=in_specs,
        out_specs=pl.BlockSpec((tq, d), lambda bb, j: (row(bb, j), 0)),
        out_shape=jax.ShapeDtypeStruct((t, d), F32),
        scratch_shapes=[pltpu.VMEM((tq, wq), BF16)],
        compiler_params=_params(("parallel", "parallel")),
        name="swa_mix",
    )(sinks, z, z, z, z, z, z, x2, km, vm, wo)


def _ffn_kernel(x_ref, gain_ref, wg_ref, wu_ref, wd_ref, out_ref, h_s, acc_s):
    c = pl.program_id(1)

    @pl.when(c == 0)
    def _():
        x = x_ref[...]
        h_s[...] = _rms(x, gain_ref[...]).astype(BF16)
        acc_s[...] = x

    h = h_s[...]
    a = (_silu(_dot(h, wg_ref[...])) * _dot(h, wu_ref[...])).astype(BF16)
    acc_s[...] += _dot(a, wd_ref[...])

    @pl.when(c == pl.num_programs(1) - 1)
    def _():
        out_ref[...] = acc_s[...]


def _ffn(x2, gain, w_gu, w_down, tm, ffc):
    t, d = x2.shape
    d_ff = w_down.shape[0]
    nc = d_ff // ffc
    return pl.pallas_call(
        _ffn_kernel,
        grid=(t // tm, nc),
        in_specs=[
            pl.BlockSpec((tm, d), lambda i, c: (i, 0)),
            pl.BlockSpec((1, d), lambda i, c: (0, 0)),
            pl.BlockSpec((d, ffc), lambda i, c: (0, c)),
            pl.BlockSpec((d, ffc), lambda i, c: (0, nc + c)),
            pl.BlockSpec((ffc, d), lambda i, c: (c, 0)),
        ],
        out_specs=pl.BlockSpec((tm, d), lambda i, c: (i, 0)),
        out_shape=jax.ShapeDtypeStruct((t, d), F32),
        scratch_shapes=[pltpu.VMEM((tm, d), BF16), pltpu.VMEM((tm, d), F32)],
        compiler_params=_params(("parallel", "arbitrary")),
        name="ffn_dense",
    )(x2, gain.reshape(1, d), w_gu, w_gu, w_down)


def _router_kernel(x_ref, gain_ref, rw_ref, h_ref, ri_ref, rg_ref, cnt_ref, carry_s):
    @pl.when(pl.program_id(0) == 0)
    def _():
        carry_s[...] = jnp.zeros_like(carry_s)

    tb = x_ref.shape[0]
    h = _rms(x_ref[...], gain_ref[...])
    h_ref[...] = h
    logits = jnp.dot(h, rw_ref[...], precision=lax.Precision.HIGHEST, preferred_element_type=F32)
    lane = lax.broadcasted_iota(jnp.int32, logits.shape, 1)
    logits = jnp.where(lane < N_EXPERTS, logits, -jnp.inf)
    m1 = jnp.max(logits, axis=-1, keepdims=True)
    i1 = jnp.min(jnp.where(logits == m1, lane, LANES), axis=-1, keepdims=True)
    rest = jnp.where(lane == i1, -jnp.inf, logits)
    m2 = jnp.max(rest, axis=-1, keepdims=True)
    i2 = jnp.min(jnp.where(rest == m2, lane, LANES), axis=-1, keepdims=True)
    e2 = jnp.exp(m2 - m1)
    g1 = 1.0 / (1.0 + e2)
    g2 = e2 / (1.0 + e2)

    sel1 = lane == i1
    sel2 = lane == i2
    onehot = jnp.where(jnp.logical_or(sel1, sel2), 1.0, 0.0)
    ti = lax.broadcasted_iota(jnp.int32, (tb, tb), 0)
    tj = lax.broadcasted_iota(jnp.int32, (tb, tb), 1)
    before = jnp.where(tj < ti, 1.0, 0.0).astype(BF16)
    seen = carry_s[...] + _dot(before, onehot.astype(BF16))
    rank1 = jnp.sum(jnp.where(sel1, seen, 0.0), axis=-1, keepdims=True).astype(jnp.int32)
    rank2 = jnp.sum(jnp.where(sel2, seen, 0.0), axis=-1, keepdims=True).astype(jnp.int32)
    carry_s[...] += jnp.sum(onehot, axis=0, keepdims=True)
    cnt_ref[...] = carry_s[...].astype(jnp.int32)

    ri = jnp.where(lane == 0, i1, jnp.where(lane == 1, i2, jnp.where(lane == 2, rank1, rank2)))
    rg = jnp.where(lane == 0, g1, g2)
    ri_ref[...] = ri[:, :8]
    rg_ref[...] = rg[:, :8]


def _router(x2, gain, rw, tb):
    t, d = x2.shape
    return pl.pallas_call(
        _router_kernel,
        grid=(t // tb,),
        in_specs=[
            pl.BlockSpec((tb, d), lambda i: (i, 0)),
            pl.BlockSpec((1, d), lambda i: (0, 0)),
            pl.BlockSpec((d, LANES), lambda i: (0, 0)),
        ],
        out_specs=[
            pl.BlockSpec((tb, d), lambda i: (i, 0)),
            pl.BlockSpec((tb, 8), lambda i: (i, 0)),
            pl.BlockSpec((tb, 8), lambda i: (i, 0)),
            pl.BlockSpec((1, LANES), lambda i: (0, 0)),
        ],
        out_shape=[
            jax.ShapeDtypeStruct((t, d), F32),
            jax.ShapeDtypeStruct((t, 8), jnp.int32),
            jax.ShapeDtypeStruct((t, 8), F32),
            jax.ShapeDtypeStruct((1, LANES), jnp.int32),
        ],
        scratch_shapes=[pltpu.VMEM((1, LANES), F32)],
        compiler_params=_params(("arbitrary",)),
        name="router",
    )(x2, gain.reshape(1, d), rw)


def _dispatch_kernel(pad_lo_ref, pad_hi_ref, dest_ref, src_ref, dst_ref, sem):
    gb = dest_ref.shape[2] // 2
    base = pl.program_id(0) * gb

    def issue(t, carry):
        row = src_ref.at[pl.ds(base + t, 1)]
        pltpu.make_async_copy(row, dst_ref.at[pl.ds(dest_ref[0, 0, 2 * t], 1)], sem).start()
        pltpu.make_async_copy(row, dst_ref.at[pl.ds(dest_ref[0, 0, 2 * t + 1], 1)], sem).start()
        return carry

    lax.fori_loop(0, gb, issue, 0)
    pltpu.make_async_copy(dst_ref.at[pl.ds(0, 2 * gb)], dst_ref.at[pl.ds(0, 2 * gb)], sem).wait()

    @pl.when(pl.program_id(0) == pl.num_programs(0) - 1)
    def _():
        for e in range(N_EXPERTS):
            lo, hi = pad_lo_ref[e], pad_hi_ref[e]

            def fill(r, carry):
                pltpu.make_async_copy(src_ref.at[pl.ds(0, 1)], dst_ref.at[pl.ds(r, 1)], sem).start()
                return carry

            def drain(r, carry):
                pltpu.make_async_copy(src_ref.at[pl.ds(0, 1)], dst_ref.at[pl.ds(r, 1)], sem).wait()
                return carry

            lax.fori_loop(lo, hi, fill, 0)
            lax.fori_loop(lo, hi, drain, 0)


def _dispatch(src, dest, pad_lo, pad_hi, n_rows, gb):
    t, d = src.shape
    return pl.pallas_call(
        _dispatch_kernel,
        grid_spec=pltpu.PrefetchScalarGridSpec(
            num_scalar_prefetch=2,
            grid=(t // gb,),
            in_specs=[
                pl.BlockSpec((1, 1, 2 * gb), lambda i, lo, hi: (i, 0, 0), memory_space=pltpu.SMEM),
                pl.BlockSpec(memory_space=pl.ANY),
            ],
            out_specs=pl.BlockSpec(memory_space=pl.ANY),
            scratch_shapes=[pltpu.SemaphoreType.DMA(())],
        ),
        out_shape=jax.ShapeDtypeStruct((n_rows, d), src.dtype),
        compiler_params=_params(("arbitrary",)),
        name="moe_dispatch",
    )(pad_lo, pad_hi, dest.reshape(t // gb, 1, 2 * gb), src)


def _collect_kernel(dest_ref, src_ref, dst_ref, sem):
    gb = dest_ref.shape[2] // 2
    base = pl.program_id(0) * gb

    def issue(t, carry):
        for k in range(2):
            pltpu.make_async_copy(src_ref.at[pl.ds(dest_ref[0, 0, 2 * t + k], 1)],
                                  dst_ref.at[k, pl.ds(base + t, 1)], sem).start()
        return carry

    lax.fori_loop(0, gb, issue, 0)
    pltpu.make_async_copy(src_ref.at[pl.ds(0, 2 * gb)], src_ref.at[pl.ds(0, 2 * gb)], sem).wait()


def _collect(src, dest, t, gb):
    d = src.shape[1]
    return pl.pallas_call(
        _collect_kernel,
        grid=(t // gb,),
        in_specs=[
            pl.BlockSpec((1, 1, 2 * gb), lambda i: (i, 0, 0), memory_space=pltpu.SMEM),
            pl.BlockSpec(memory_space=pl.ANY),
        ],
        out_specs=pl.BlockSpec(memory_space=pl.ANY),
        out_shape=jax.ShapeDtypeStruct((2, t, d), src.dtype),
        scratch_shapes=[pltpu.SemaphoreType.DMA(())],
        compiler_params=_params(("arbitrary",)),
        name="moe_collect",
    )(dest.reshape(t // gb, 1, 2 * gb), src)


def _moe_kernel(te_ref, nu_ref, xs_ref, wg_ref, wu_ref, wd_ref, out_ref, h_s, acc_s):
    i = pl.program_id(0)
    c = pl.program_id(1)
    last = pl.num_programs(1) - 1
    used = i < nu_ref[0]

    @pl.when(used)
    def _():
        @pl.when(c == 0)
        def _():
            h_s[...] = xs_ref[...].astype(BF16)
            acc_s[...] = jnp.zeros_like(acc_s)

        h = h_s[...]
        a = (_silu(_dot(h, wg_ref[0])) * _dot(h, wu_ref[0])).astype(BF16)
        acc_s[...] += _dot(a, wd_ref[0])

        @pl.when(c == last)
        def _():
            out_ref[...] = acc_s[...]

    @pl.when(jnp.logical_and(jnp.logical_not(used), c == last))
    def _():
        out_ref[...] = jnp.zeros_like(out_ref)


def _moe(xs, tile_expert, n_used, w_gu, w_down, tm, ffc):
    n_rows, d = xs.shape
    d_ff = w_down.shape[1]
    nc = d_ff // ffc
    nt = n_rows // tm

    def chunk(i, c, nu):
        return jnp.where(i < nu[0], c, nc - 1)

    return pl.pallas_call(
        _moe_kernel,
        grid_spec=pltpu.PrefetchScalarGridSpec(
            num_scalar_prefetch=2,
            grid=(nt, nc),
            in_specs=[
                pl.BlockSpec((tm, d), lambda i, c, te, nu: (jnp.minimum(i, nu[0] - 1), 0)),
                pl.BlockSpec((1, d, ffc), lambda i, c, te, nu: (te[i], 0, chunk(i, c, nu))),
                pl.BlockSpec((1, d, ffc), lambda i, c, te, nu: (te[i], 0, nc + chunk(i, c, nu))),
                pl.BlockSpec((1, ffc, d), lambda i, c, te, nu: (te[i], chunk(i, c, nu), 0)),
            ],
            out_specs=pl.BlockSpec((tm, d), lambda i, c, te, nu: (i, 0)),
            scratch_shapes=[pltpu.VMEM((tm, d), BF16), pltpu.VMEM((tm, d), F32)],
        ),
        out_shape=jax.ShapeDtypeStruct((n_rows, d), F32),
        compiler_params=_params(("arbitrary", "arbitrary")),
        name="moe_experts",
    )(tile_expert, n_used, xs, w_gu, w_gu, w_down)


def _combine_kernel(x_ref, ya_ref, yb_ref, rg_ref, gain_ref, out_ref):
    g = rg_ref[...]
    y = g[:, 0:1] * ya_ref[...] + g[:, 1:2] * yb_ref[...]
    out_ref[...] = _rms(x_ref[...] + y, gain_ref[...])


def _combine(x2, y2, rg, gain, tb):
    t, d = x2.shape
    return pl.pallas_call(
        _combine_kernel,
        grid=(t // tb,),
        in_specs=[
            pl.BlockSpec((tb, d), lambda i: (i, 0)),
            pl.BlockSpec((None, tb, d), lambda i: (0, i, 0)),
            pl.BlockSpec((None, tb, d), lambda i: (1, i, 0)),
            pl.BlockSpec((tb, 8), lambda i: (i, 0)),
            pl.BlockSpec((1, d), lambda i: (0, 0)),
        ],
        out_specs=pl.BlockSpec((tb, d), lambda i: (i, 0)),
        out_shape=jax.ShapeDtypeStruct((t, d), F32),
        compiler_params=_params(("parallel",)),
        name="moe_combine",
    )(x2, y2, y2, rg, gain.reshape(1, d))


def _pad_heads(w, heads, width, padded):
    lead = w.shape[:-1]
    w = w.reshape(lead + (heads, width))
    w = jnp.pad(w, [(0, 0)] * len(lead) + [(0, 0), (0, padded - width)])
    return w.reshape(lead + (heads * padded,))


def _layer_a_weights(w_in, gate_w, gate_b, out_norm, w_out):
    dqk = GLA_HEADS * GLA_DK
    dv = GLA_HEADS * GLA_DV
    o = 0
    wq = w_in[:, o:o + dqk]; o += dqk
    wk = w_in[:, o:o + dqk]; o += dqk
    wv = w_in[:, o:o + dv]; o += dv
    wl = w_in[:, o:o + GLA_LOWRANK]; o += GLA_LOWRANK
    wr = w_in[:, o:o + dv]; o += dv
    wm = w_in[:, o:o + MEM_WIDTH]
    w = jnp.concatenate([
        _pad_heads(wq, GLA_HEADS, GLA_DK, GLA_DK_PAD),
        _pad_heads(wk, GLA_HEADS, GLA_DK, GLA_DK_PAD),
        _pad_heads(wv, GLA_HEADS, GLA_DV, GLA_DV_PAD),
        _pad_heads(wr, GLA_HEADS, GLA_DV, GLA_DV_PAD),
        wm,
        jnp.pad(wl, ((0, 0), (0, LANES - GLA_LOWRANK))),
    ], axis=1).astype(BF16)
    gw = _pad_heads(gate_w, GLA_HEADS, GLA_DK, GLA_DK_PAD)
    gw = jnp.pad(gw, ((0, LANES - GLA_LOWRANK), (0, 0))).astype(BF16)
    gb = _pad_heads(gate_b, GLA_HEADS, GLA_DK, GLA_DK_PAD).reshape(1, -1)
    og = jnp.pad(out_norm, (0, GLA_DV_PAD - GLA_DV)).reshape(1, -1)
    wo_main = w_out[:dv].reshape(GLA_HEADS, GLA_DV, -1)
    wo_main = jnp.pad(wo_main, ((0, 0), (0, GLA_DV_PAD - GLA_DV), (0, 0))).reshape(GLA_HEADS * GLA_DV_PAD, -1)
    wo = jnp.concatenate([wo_main, w_out[dv:]], axis=0).astype(BF16)
    return w, gw, gb, og, wo


def kernel(x, mem, positions, mix_norm, w_in_a, gla_gate_w, gla_gate_b, gla_out_norm, w_in_b, swa_sinks,
           mem_norm, w_mem_kv, w_out, ffn_norm, ffn_w_gate_up, ffn_w_down, router_w, exp_w_gate_up,
           exp_w_down, final_norm):
    b, s, d = x.shape
    t = b * s
    x2 = x.reshape(t, d)
    tq = min(512, s)
    tm_moe = 512
    gb = min(2048, t)

    km, vm = _memkv(mem, mem_norm, w_mem_kv)

    wa, gw, gbias, og, wo_a = _layer_a_weights(w_in_a[0], gla_gate_w[0], gla_gate_b[0], gla_out_norm[0], w_out[0])
    z = _proj(x2, mix_norm[0], wa, tq)
    x2 = _gla_mix(z, x2, gw, gbias, og, km[0], vm[0], wo_a, b, s, tq)
    x2 = _ffn(x2, ffn_norm[0], ffn_w_gate_up[0].astype(BF16), ffn_w_down[0].astype(BF16), tq, 1408)

    n_q = SWA_HEADS * HEAD_DIM
    n_kv = SWA_KV_HEADS * HEAD_DIM
    wb = jnp.concatenate([w_in_b[0][:, :n_q] * (HEAD_DIM ** -0.5), w_in_b[0][:, n_q:]], axis=1).astype(BF16)
    half = ROPE_DIM // 2
    inv_freq = jnp.power(jnp.float32(ROPE_THETA), -jnp.arange(half, dtype=F32) / half)
    invf = jnp.tile(inv_freq, LANES // half).reshape(1, LANES)
    pos = positions.astype(F32).reshape(t, 1)
    z = _proj(x2, mix_norm[1], wb, tq, rope=(pos, invf, n_q + n_kv))
    x2 = _swa_mix(z, x2, swa_sinks[0], km[1], vm[1], w_out[1].astype(BF16), b, s, tq)

    rw = jnp.pad(router_w[0], ((0, 0), (0, LANES - N_EXPERTS)))
    h, ri, rg, counts = _router(x2, ffn_norm[1], rw, tq)

    cnt = counts[0, :N_EXPERTS]
    padded = ((cnt + tm_moe - 1) // tm_moe) * tm_moe
    ends = jnp.cumsum(padded)
    offs = ends - padded
    n_tiles = 2 * t // tm_moe + N_EXPERTS
    n_used = (ends[-1] // tm_moe).astype(jnp.int32)
    tile_start = jnp.arange(n_tiles, dtype=jnp.int32) * tm_moe
    tile_expert = jnp.sum(tile_start[:, None] >= ends[None, :], axis=1).astype(jnp.int32)
    tile_expert = jnp.minimum(tile_expert, N_EXPERTS - 1)
    last_expert = tile_expert[jnp.maximum(n_used - 1, 0)]
    tile_expert = jnp.where(jnp.arange(n_tiles) < n_used, tile_expert, last_expert)
    experts = ri[:, 0:2]
    off_of = jnp.sum(jnp.where(experts[:, :, None] == jnp.arange(N_EXPERTS)[None, None, :],
                               offs[None, None, :], 0), axis=-1)
    dest = (off_of + ri[:, 2:4]).astype(jnp.int32)

    pad_hi = ends.at[N_EXPERTS - 1].set(n_tiles * tm_moe)
    xs = _dispatch(h, dest, (offs + cnt).astype(jnp.int32), pad_hi.astype(jnp.int32), n_tiles * tm_moe, gb)
    ys = _moe(xs, tile_expert, n_used.reshape(1), exp_w_gate_up[0].astype(BF16), exp_w_down[0].astype(BF16),
              tm_moe, 896)
    y2 = _collect(ys, dest, t, gb)
    out = _combine(x2, y2, rg, final_norm, tq)
    return out.reshape(b, s, d)
```

```python
import functools

import jax
import jax.numpy as jnp
from jax import lax
from jax.experimental import pallas as pl
from jax.experimental.pallas import tpu as pltpu

F32 = jnp.float32
BF16 = jnp.bfloat16

HEAD_DIM = 64
MEM_HEADS = 4
MEM_WIDTH = MEM_HEADS * HEAD_DIM
GLA_HEADS = 4
GLA_DK = 96
GLA_DV = 192
GLA_LOWRANK = 16
GLA_TAU = 16.0
GLA_CHUNK = 64
SWA_HEADS = 12
SWA_KV_HEADS = 4
WINDOW = 128
ROPE_DIM = 16
ROPE_THETA = 500000.0
N_EXPERTS = 8
EPS = 1e-6

LANES = 128
TILE_ROWS = 8
GLA_DK_PAD = 128
GLA_DV_PAD = 256
VMEM_LIMIT = 56 * 1024 * 1024

NT_DIMS = (((1,), (1,)), ((), ()))
TN_DIMS = (((0,), (0,)), ((), ()))


def _params(sem):
    return pltpu.CompilerParams(dimension_semantics=sem, vmem_limit_bytes=VMEM_LIMIT)


def _rms(xf, gain):
    return xf * lax.rsqrt(jnp.mean(xf * xf, axis=-1, keepdims=True) + EPS) * gain


def _dot(a, b):
    return jnp.dot(a, b, preferred_element_type=F32)


def _silu(x):
    return x * jax.nn.sigmoid(x)


def _store_token_tiles(ref, val):
    n = val.shape[0]
    for c in range(TILE_ROWS):
        ref[pl.ds(c, n, stride=TILE_ROWS), :] = val[:, c * LANES:(c + 1) * LANES].astype(ref.dtype)


def _load_token_tiles(ref, n):
    return [ref[pl.ds(c, n, stride=TILE_ROWS), :] for c in range(TILE_ROWS)]


def _memkv_kernel(mem_ref, gain_ref, w_ref, km_ref, vm_ref):
    mem_n = _rms(mem_ref[0], gain_ref[0]).astype(BF16)
    kv = _dot(mem_n, w_ref[0].astype(BF16))
    k = kv[:, :MEM_WIDTH] * (HEAD_DIM ** -0.5)
    v = kv[:, MEM_WIDTH:]
    head = lax.broadcasted_iota(jnp.int32, k.shape, 1) // HEAD_DIM
    for h in range(MEM_HEADS):
        km_ref[0, 0, h] = jnp.where(head == h, k, 0.0).astype(BF16)
        vm_ref[0, 0, h] = jnp.where(head == h, v, 0.0).astype(BF16)


def _memkv(mem, mem_norm, w_mem_kv):
    depth = mem_norm.shape[0]
    b, n_mem, d = mem.shape
    out = jax.ShapeDtypeStruct((depth, b, MEM_HEADS, n_mem, MEM_WIDTH), BF16)
    blk = pl.BlockSpec((1, 1, MEM_HEADS, n_mem, MEM_WIDTH), lambda i, bb: (i, bb, 0, 0, 0))
    return pl.pallas_call(
        _memkv_kernel,
        grid=(depth, b),
        in_specs=[
            pl.BlockSpec((1, n_mem, d), lambda i, bb: (bb, 0, 0)),
            pl.BlockSpec((1, 1, d), lambda i, bb: (i, 0, 0)),
            pl.BlockSpec((1, d, 2 * MEM_WIDTH), lambda i, bb: (i, 0, 0)),
        ],
        out_specs=[blk, blk],
        out_shape=[out, out],
        compiler_params=_params(("arbitrary", "arbitrary")),
        name="mem_kv",
    )(mem, mem_norm.reshape(depth, 1, d), w_mem_kv)


def _mem_attention(qm, km_ref, vm_ref):
    acc = jnp.zeros((qm.shape[0], MEM_WIDTH), F32)
    for h in range(MEM_HEADS):
        s = lax.dot_general(qm, km_ref[h], NT_DIMS, preferred_element_type=F32)
        p = jnp.exp(s - jnp.max(s, axis=-1, keepdims=True))
        p = p / jnp.sum(p, axis=-1, keepdims=True)
        acc = acc + _dot(p.astype(BF16), vm_ref[h])
    return acc


def _proj_kernel(x_ref, gain_ref, w_ref, z_ref, *, col_chunk):
    h = _rms(x_ref[...], gain_ref[...]).astype(BF16)
    n = w_ref.shape[1]
    for c0 in range(0, n, col_chunk):
        z_ref[:, c0:c0 + col_chunk] = _dot(h, w_ref[:, c0:c0 + col_chunk]).astype(BF16)


def _proj_rope_kernel(x_ref, gain_ref, w_ref, pos_ref, invf_ref, z_ref, *, rope_cols):
    h = _rms(x_ref[...], gain_ref[...]).astype(BF16)
    ang = pos_ref[...] * invf_ref[...]
    lane = lax.broadcasted_iota(jnp.int32, ang.shape, 1) % HEAD_DIM
    half = ROPE_DIM // 2
    cosf = jnp.where(lane < ROPE_DIM, jnp.cos(ang), 1.0)
    sn = jnp.sin(ang)
    sinf = jnp.where(lane < half, -sn, jnp.where(lane < ROPE_DIM, sn, 0.0))
    first = lane < half
    n = w_ref.shape[1]
    for c0 in range(0, n, LANES):
        zc = _dot(h, w_ref[:, c0:c0 + LANES])
        if c0 < rope_cols:
            partner = jnp.where(first, pltpu.roll(zc, LANES - half, 1), pltpu.roll(zc, half, 1))
            zc = zc * cosf + partner * sinf
        z_ref[:, c0:c0 + LANES] = zc.astype(BF16)


def _proj(x2, gain, w, tm, rope=None):
    t, d = x2.shape
    n = w.shape[1]
    in_specs = [
        pl.BlockSpec((tm, d), lambda i: (i, 0)),
        pl.BlockSpec((1, d), lambda i: (0, 0)),
        pl.BlockSpec((d, n), lambda i: (0, 0)),
    ]
    args = [x2, gain.reshape(1, d), w]
    if rope is None:
        kern = functools.partial(_proj_kernel, col_chunk=n // 3)
    else:
        pos, invf, rope_cols = rope
        kern = functools.partial(_proj_rope_kernel, rope_cols=rope_cols)
        in_specs += [pl.BlockSpec((tm, 1), lambda i: (i, 0)), pl.BlockSpec((1, LANES), lambda i: (0, 0))]
        args += [pos, invf]
    return pl.pallas_call(
        kern,
        grid=(t // tm,),
        in_specs=in_specs,
        out_specs=pl.BlockSpec((tm, n), lambda i: (i, 0)),
        out_shape=jax.ShapeDtypeStruct((t, n), BF16),
        compiler_params=_params(("parallel",)),
        name="in_proj" if rope is None else "in_proj_rope",
    )(*args)


def _gla_mix_kernel(q_ref, k_ref, v_ref, r_ref, qm_ref, gl_ref, x_ref, gw_ref, gb_ref, on_ref,
                    km_ref, vm_ref, wo_ref, out_ref, la_s, main_s, st_s):
    @pl.when(pl.program_id(1) == 0)
    def _():
        st_s[...] = jnp.zeros_like(st_s)

    tq = q_ref.shape[0]
    c_len = GLA_CHUNK
    logits = _dot(gl_ref[...], gw_ref[...]) + gb_ref[...]
    log_sig = jnp.minimum(logits, 0.0) - jnp.log(1.0 + jnp.exp(-jnp.abs(logits)))
    lane = lax.broadcasted_iota(jnp.int32, logits.shape, 1) % GLA_DK_PAD
    la_s[...] = jnp.where(lane < GLA_DK, log_sig * (1.0 / GLA_TAU), 0.0)

    row = lax.broadcasted_iota(jnp.int32, (c_len, c_len), 0)
    col = lax.broadcasted_iota(jnp.int32, (c_len, c_len), 1)
    causal = col <= row
    tri = causal.astype(F32)
    scale = GLA_DK ** -0.5

    def chunk(c, carry):
        r0 = pl.multiple_of(c * c_len, c_len)
        rows = pl.ds(r0, c_len)
        for h in range(GLA_HEADS):
            kcols = slice(h * GLA_DK_PAD, (h + 1) * GLA_DK_PAD)
            vcols = slice(h * GLA_DV_PAD, (h + 1) * GLA_DV_PAD)
            la = la_s[rows, kcols]
            b = jnp.dot(tri, la, precision=lax.Precision.HIGHEST, preferred_element_type=F32)
            b_last = b[c_len - 1:c_len, :]
            qf = q_ref[rows, kcols].astype(F32)
            kf = k_ref[rows, kcols].astype(F32)
            qd = (qf * scale * jnp.exp(b)).astype(BF16)
            kd = (kf * jnp.exp(-b)).astype(BF16)
            ke = (kf * jnp.exp(b_last - b)).astype(BF16)
            vc = v_ref[rows, vcols]
            attn = lax.dot_general(qd, kd, NT_DIMS, preferred_element_type=F32)
            attn = jnp.where(causal, attn, 0.0).astype(BF16)
            st = st_s[h]
            o = _dot(attn, vc) + lax.dot_general(qd, st.astype(BF16), NT_DIMS, preferred_element_type=F32)
            st_s[h] = st * jnp.exp(b_last) + lax.dot_general(vc, ke, TN_DIMS, preferred_element_type=F32)
            ms = jnp.sum(o * o, axis=-1, keepdims=True) * (1.0 / GLA_DV)
            o_n = o * lax.rsqrt(ms + EPS) * on_ref[...]
            main_s[rows, vcols] = (o_n * _silu(r_ref[rows, vcols].astype(F32))).astype(BF16)
        return carry

    lax.fori_loop(0, tq // c_len, chunk, 0, unroll=2)

    mem_out = _mem_attention(qm_ref[...], km_ref, vm_ref)
    n_main = GLA_HEADS * GLA_DV_PAD
    mixed = _dot(main_s[...], wo_ref[0:n_main, :]) + _dot(mem_out.astype(BF16), wo_ref[n_main:, :])
    out_ref[...] = x_ref[...] + mixed


def _gla_mix(z, x2, gate_w, gate_b, out_gain, km, vm, wo, b, s, tq):
    t, d = x2.shape
    nj = s // tq
    wq = GLA_HEADS * GLA_DK_PAD
    wv = GLA_HEADS * GLA_DV_PAD
    row = lambda bb, j: bb * nj + j
    qm_blk = (2 * wq + 2 * wv) // MEM_WIDTH
    gl_blk = (2 * wq + 2 * wv + MEM_WIDTH) // LANES
    n_mem = km.shape[2]
    in_specs = [
        pl.BlockSpec((tq, wq), lambda bb, j: (row(bb, j), 0)),
        pl.BlockSpec((tq, wq), lambda bb, j: (row(bb, j), 1)),
        pl.BlockSpec((tq, wv), lambda bb, j: (row(bb, j), 1)),
        pl.BlockSpec((tq, wv), lambda bb, j: (row(bb, j), 2)),
        pl.BlockSpec((tq, MEM_WIDTH), lambda bb, j: (row(bb, j), qm_blk)),
        pl.BlockSpec((tq, LANES), lambda bb, j: (row(bb, j), gl_blk)),
        pl.BlockSpec((tq, d), lambda bb, j: (row(bb, j), 0)),
        pl.BlockSpec(gate_w.shape, lambda bb, j: (0, 0)),
        pl.BlockSpec(gate_b.shape, lambda bb, j: (0, 0)),
        pl.BlockSpec(out_gain.shape, lambda bb, j: (0, 0)),
        pl.BlockSpec((None, MEM_HEADS, n_mem, MEM_WIDTH), lambda bb, j: (bb, 0, 0, 0)),
        pl.BlockSpec((None, MEM_HEADS, n_mem, MEM_WIDTH), lambda bb, j: (bb, 0, 0, 0)),
        pl.BlockSpec(wo.shape, lambda bb, j: (0, 0)),
    ]
    return pl.pallas_call(
        _gla_mix_kernel,
        grid=(b, nj),
        in_specs=in_specs,
        out_specs=pl.BlockSpec((tq, d), lambda bb, j: (row(bb, j), 0)),
        out_shape=jax.ShapeDtypeStruct((t, d), F32),
        scratch_shapes=[
            pltpu.VMEM((tq, wq), F32),
            pltpu.VMEM((tq, wv), BF16),
            pltpu.VMEM((GLA_HEADS, GLA_DV_PAD, GLA_DK_PAD), F32),
        ],
        compiler_params=_params(("arbitrary", "arbitrary")),
        name="gla_mix",
    )(z, z, z, z, z, z, x2, gate_w, gate_b, out_gain, km, vm, wo)


def _swa_mix_kernel(sink_ref, q_ref, k_ref, v_ref, kp_ref, vp_ref, qm_ref, x_ref, km_ref, vm_ref,
                    wo_ref, out_ref, main_s):
    has_prev = pl.program_id(1) > 0
    tq = q_ref.shape[0]
    w = WINDOW
    qi = lax.broadcasted_iota(jnp.int32, (w, w), 0)
    kj = lax.broadcasted_iota(jnp.int32, (w, w), 1)
    hi_valid = kj <= qi
    lo_band = kj > qi
    group = SWA_HEADS // SWA_KV_HEADS
    neg_inf = -jnp.inf
    for sub in range(tq // w):
        r0 = sub * w
        if sub == 0:
            k_lo, v_lo = kp_ref[...], vp_ref[...]
            lo_valid = jnp.logical_and(lo_band, has_prev)
        else:
            k_lo, v_lo = k_ref[r0 - w:r0, :], v_ref[r0 - w:r0, :]
            lo_valid = lo_band
        k_hi, v_hi = k_ref[r0:r0 + w, :], v_ref[r0:r0 + w, :]
        for h in range(SWA_HEADS):
            g = h // group
            gc = slice(g * HEAD_DIM, (g + 1) * HEAD_DIM)
            hc = slice(h * HEAD_DIM, (h + 1) * HEAD_DIM)
            qh = q_ref[r0:r0 + w, hc]
            s_lo = lax.dot_general(qh, k_lo[:, gc], NT_DIMS, preferred_element_type=F32)
            s_hi = lax.dot_general(qh, k_hi[:, gc], NT_DIMS, preferred_element_type=F32)
            s_lo = jnp.where(lo_valid, s_lo, neg_inf)
            s_hi = jnp.where(hi_valid, s_hi, neg_inf)
            sink = sink_ref[h]
            m = jnp.maximum(jnp.max(s_lo, axis=-1, keepdims=True), jnp.max(s_hi, axis=-1, keepdims=True))
            m = jnp.maximum(m, sink)
            p_lo = jnp.exp(s_lo - m)
            p_hi = jnp.exp(s_hi - m)
            den = (jnp.sum(p_lo, axis=-1, keepdims=True) + jnp.sum(p_hi, axis=-1, keepdims=True)
                   + jnp.exp(sink - m))
            o = _dot(p_lo.astype(BF16), v_lo[:, gc]) + _dot(p_hi.astype(BF16), v_hi[:, gc])
            main_s[r0:r0 + w, hc] = (o / den).astype(BF16)

    mem_out = _mem_attention(qm_ref[...], km_ref, vm_ref)
    n_main = SWA_HEADS * HEAD_DIM
    mixed = _dot(main_s[...], wo_ref[0:n_main, :]) + _dot(mem_out.astype(BF16), wo_ref[n_main:, :])
    out_ref[...] = x_ref[...] + mixed


def _swa_mix(z, x2, sinks, km, vm, wo, b, s, tq):
    t, d = x2.shape
    nj = s // tq
    wq = SWA_HEADS * HEAD_DIM
    wkv = SWA_KV_HEADS * HEAD_DIM
    row = lambda bb, j: bb * nj + j
    per = tq // WINDOW
    prev = lambda bb, j: jnp.maximum(row(bb, j) * per - 1, 0)
    k_blk = wq // wkv
    n_mem = km.shape[2]
    in_specs = [
        pl.BlockSpec(memory_space=pltpu.SMEM),
        pl.BlockSpec((tq, wq), lambda bb, j: (row(bb, j), 0)),
        pl.BlockSpec((tq, wkv), lambda bb, j: (row(bb, j), k_blk)),
        pl.BlockSpec((tq, wkv), lambda bb, j: (row(bb, j), k_blk + 1)),
        pl.BlockSpec((WINDOW, wkv), lambda bb, j: (prev(bb, j), k_blk)),
        pl.BlockSpec((WINDOW, wkv), lambda bb, j: (prev(bb, j), k_blk + 1)),
        pl.BlockSpec((tq, MEM_WIDTH), lambda bb, j: (row(bb, j), k_blk + 2)),
        pl.BlockSpec((tq, d), lambda bb, j: (row(bb, j), 0)),
        pl.BlockSpec((None, MEM_HEADS, n_mem, MEM_WIDTH), lambda bb, j: (bb, 0, 0, 0)),
        pl.BlockSpec((None, MEM_HEADS, n_mem, MEM_WIDTH), lambda bb, j: (bb, 0, 0, 0)),
        pl.BlockSpec(wo.shape, lambda bb, j: (0, 0)),
    ]
    return pl.pallas_call(
        _swa_mix_kernel,
        grid=(b, nj),
        in_specs=in_specs,
        out_specs=pl.BlockSpec((tq, d), lambda bb, j: (row(bb, j), 0)),
        out_shape=jax.ShapeDtypeStruct((t, d), F32),
        scratch_shapes=[pltpu.VMEM((tq, wq), BF16)],
        compiler_params=_params(("parallel", "parallel")),
        name="swa_mix",
    )(sinks, z, z, z, z, z, z, x2, km, vm, wo)


def _ffn_kernel(x_ref, gain_ref, wg_ref, wu_ref, wd_ref, out_ref, h_s, acc_s):
    c = pl.program_id(1)

    @pl.when(c == 0)
    def _():
        x = x_ref[...]
        h_s[...] = _rms(x, gain_ref[...]).astype(BF16)
        acc_s[...] = x

    h = h_s[...]
    a = (_silu(_dot(h, wg_ref[...])) * _dot(h, wu_ref[...])).astype(BF16)
    acc_s[...] += _dot(a, wd_ref[...])

    @pl.when(c == pl.num_programs(1) - 1)
    def _():
        out_ref[...] = acc_s[...]


def _ffn(x2, gain, w_gu, w_down, tm, ffc):
    t, d = x2.shape
    d_ff = w_down.shape[0]
    nc = d_ff // ffc
    return pl.pallas_call(
        _ffn_kernel,
        grid=(t // tm, nc),
        in_specs=[
            pl.BlockSpec((tm, d), lambda i, c: (i, 0)),
            pl.BlockSpec((1, d), lambda i, c: (0, 0)),
            pl.BlockSpec((d, ffc), lambda i, c: (0, c)),
            pl.BlockSpec((d, ffc), lambda i, c: (0, nc + c)),
            pl.BlockSpec((ffc, d), lambda i, c: (c, 0)),
        ],
        out_specs=pl.BlockSpec((tm, d), lambda i, c: (i, 0)),
        out_shape=jax.ShapeDtypeStruct((t, d), F32),
        scratch_shapes=[pltpu.VMEM((tm, d), BF16), pltpu.VMEM((tm, d), F32)],
        compiler_params=_params(("parallel", "arbitrary")),
        name="ffn_dense",
    )(x2, gain.reshape(1, d), w_gu, w_gu, w_down)


def _router_kernel(x_ref, gain_ref, rw_ref, h_ref, ri_ref, rg_ref, cnt_ref, carry_s):
    @pl.when(pl.program_id(0) == 0)
    def _():
        carry_s[...] = jnp.zeros_like(carry_s)

    tb = x_ref.shape[0]
    h = _rms(x_ref[...], gain_ref[...])
    _store_token_tiles(h_ref, h)
    logits = jnp.dot(h, rw_ref[...], precision=lax.Precision.HIGHEST, preferred_element_type=F32)
    lane = lax.broadcasted_iota(jnp.int32, logits.shape, 1)
    logits = jnp.where(lane < N_EXPERTS, logits, -jnp.inf)
    m1 = jnp.max(logits, axis=-1, keepdims=True)
    i1 = jnp.min(jnp.where(logits == m1, lane, LANES), axis=-1, keepdims=True)
    rest = jnp.where(lane == i1, -jnp.inf, logits)
    m2 = jnp.max(rest, axis=-1, keepdims=True)
    i2 = jnp.min(jnp.where(rest == m2, lane, LANES), axis=-1, keepdims=True)
    e2 = jnp.exp(m2 - m1)
    g1 = 1.0 / (1.0 + e2)
    g2 = e2 / (1.0 + e2)

    sel1 = lane == i1
    sel2 = lane == i2
    onehot = jnp.where(jnp.logical_or(sel1, sel2), 1.0, 0.0)
    ti = lax.broadcasted_iota(jnp.int32, (tb, tb), 0)
    tj = lax.broadcasted_iota(jnp.int32, (tb, tb), 1)
    before = jnp.where(tj < ti, 1.0, 0.0).astype(BF16)
    seen = carry_s[...] + _dot(before, onehot.astype(BF16))
    rank1 = jnp.sum(jnp.where(sel1, seen, 0.0), axis=-1, keepdims=True).astype(jnp.int32)
    rank2 = jnp.sum(jnp.where(sel2, seen, 0.0), axis=-1, keepdims=True).astype(jnp.int32)
    carry_s[...] += jnp.sum(onehot, axis=0, keepdims=True)
    cnt_ref[...] = carry_s[...].astype(jnp.int32)

    ri = jnp.where(lane == 0, i1, jnp.where(lane == 1, i2, jnp.where(lane == 2, rank1, rank2)))
    rg = jnp.where(lane == 0, g1, g2)
    ri_ref[...] = ri[:, :8]
    rg_ref[...] = rg[:, :8]


def _router(x2, gain, rw, tb):
    t, d = x2.shape
    return pl.pallas_call(
        _router_kernel,
        grid=(t // tb,),
        in_specs=[
            pl.BlockSpec((tb, d), lambda i: (i, 0)),
            pl.BlockSpec((1, d), lambda i: (0, 0)),
            pl.BlockSpec((d, LANES), lambda i: (0, 0)),
        ],
        out_specs=[
            pl.BlockSpec((tb * TILE_ROWS, LANES), lambda i: (i, 0)),
            pl.BlockSpec((tb, 8), lambda i: (i, 0)),
            pl.BlockSpec((tb, 8), lambda i: (i, 0)),
            pl.BlockSpec((1, LANES), lambda i: (0, 0)),
        ],
        out_shape=[
            jax.ShapeDtypeStruct((t * TILE_ROWS, LANES), F32),
            jax.ShapeDtypeStruct((t, 8), jnp.int32),
            jax.ShapeDtypeStruct((t, 8), F32),
            jax.ShapeDtypeStruct((1, LANES), jnp.int32),
        ],
        scratch_shapes=[pltpu.VMEM((1, LANES), F32)],
        compiler_params=_params(("arbitrary",)),
        name="router",
    )(x2, gain.reshape(1, d), rw)


def _tile_rows(i):
    return pl.ds(pl.multiple_of(i * TILE_ROWS, TILE_ROWS), TILE_ROWS)


def _dispatch_kernel(pad_lo_ref, pad_hi_ref, dest_ref, src_ref, dst_ref, sem):
    gb = dest_ref.shape[2] // 2

    def issue(t, carry):
        tile = src_ref.at[_tile_rows(t)]
        pltpu.make_async_copy(tile, dst_ref.at[_tile_rows(dest_ref[0, 0, 2 * t])], sem).start()
        pltpu.make_async_copy(tile, dst_ref.at[_tile_rows(dest_ref[0, 0, 2 * t + 1])], sem).start()
        return carry

    lax.fori_loop(0, gb, issue, 0)
    for _ in range(2):
        pltpu.make_async_copy(src_ref, dst_ref.at[pl.ds(0, gb * TILE_ROWS)], sem).wait()

    @pl.when(pl.program_id(0) == pl.num_programs(0) - 1)
    def _():
        for e in range(N_EXPERTS):
            lo, hi = pad_lo_ref[e], pad_hi_ref[e]

            def fill(r, carry):
                pltpu.make_async_copy(src_ref.at[_tile_rows(0)], dst_ref.at[_tile_rows(r)], sem).start()
                return carry

            def drain(r, carry):
                pltpu.make_async_copy(src_ref.at[_tile_rows(0)], dst_ref.at[_tile_rows(r)], sem).wait()
                return carry

            lax.fori_loop(lo, hi, fill, 0)
            lax.fori_loop(lo, hi, drain, 0)


def _dispatch(src, dest, pad_lo, pad_hi, n_rows, gb):
    t = src.shape[0] // TILE_ROWS
    return pl.pallas_call(
        _dispatch_kernel,
        grid_spec=pltpu.PrefetchScalarGridSpec(
            num_scalar_prefetch=2,
            grid=(t // gb,),
            in_specs=[
                pl.BlockSpec((1, 1, 2 * gb), lambda i, lo, hi: (i, 0, 0), memory_space=pltpu.SMEM),
                pl.BlockSpec((gb * TILE_ROWS, LANES), lambda i, lo, hi: (i, 0)),
            ],
            out_specs=pl.BlockSpec(memory_space=pl.ANY),
            scratch_shapes=[pltpu.SemaphoreType.DMA(())],
        ),
        out_shape=jax.ShapeDtypeStruct((n_rows * TILE_ROWS, LANES), src.dtype),
        compiler_params=_params(("arbitrary",)),
        name="moe_dispatch",
    )(pad_lo, pad_hi, dest.reshape(t // gb, 1, 2 * gb), src)


def _moe_kernel(te_ref, nu_ref, xs_ref, wg_ref, wu_ref, wd_ref, out_ref, h_s, acc_s):
    i = pl.program_id(0)
    c = pl.program_id(1)
    last = pl.num_programs(1) - 1
    used = i < nu_ref[0]

    @pl.when(used)
    def _():
        tm = h_s.shape[0]

        @pl.when(c == 0)
        def _():
            for cc, part in enumerate(_load_token_tiles(xs_ref, tm)):
                h_s[:, cc * LANES:(cc + 1) * LANES] = part.astype(BF16)
            acc_s[...] = jnp.zeros_like(acc_s)

        h = h_s[...]
        a = (_silu(_dot(h, wg_ref[0])) * _dot(h, wu_ref[0])).astype(BF16)
        acc_s[...] += _dot(a, wd_ref[0])

        @pl.when(c == last)
        def _():
            _store_token_tiles(out_ref, acc_s[...])

    @pl.when(jnp.logical_and(jnp.logical_not(used), c == last))
    def _():
        out_ref[...] = jnp.zeros_like(out_ref)


def _moe(xs, tile_expert, n_used, w_gu, w_down, tm, ffc):
    n_rows = xs.shape[0] // TILE_ROWS
    d = w_down.shape[2]
    d_ff = w_down.shape[1]
    nc = d_ff // ffc
    nt = n_rows // tm

    def chunk(i, c, nu):
        return jnp.where(i < nu[0], c, nc - 1)

    return pl.pallas_call(
        _moe_kernel,
        grid_spec=pltpu.PrefetchScalarGridSpec(
            num_scalar_prefetch=2,
            grid=(nt, nc),
            in_specs=[
                pl.BlockSpec((tm * TILE_ROWS, LANES), lambda i, c, te, nu: (jnp.minimum(i, nu[0] - 1), 0)),
                pl.BlockSpec((1, d, ffc), lambda i, c, te, nu: (te[i], 0, chunk(i, c, nu))),
                pl.BlockSpec((1, d, ffc), lambda i, c, te, nu: (te[i], 0, nc + chunk(i, c, nu))),
                pl.BlockSpec((1, ffc, d), lambda i, c, te, nu: (te[i], chunk(i, c, nu), 0)),
            ],
            out_specs=pl.BlockSpec((tm * TILE_ROWS, LANES), lambda i, c, te, nu: (i, 0)),
            scratch_shapes=[pltpu.VMEM((tm, d), BF16), pltpu.VMEM((tm, d), F32)],
        ),
        out_shape=jax.ShapeDtypeStruct((n_rows * TILE_ROWS, LANES), F32),
        compiler_params=_params(("arbitrary", "arbitrary")),
        name="moe_experts",
    )(tile_expert, n_used, xs, w_gu, w_gu, w_down)


def _combine_kernel(dest_ref, x_ref, rg_ref, gain_ref, ys_ref, out_ref, buf, sem):
    tb = x_ref.shape[0]

    def issue(t, carry):
        for k in range(2):
            pltpu.make_async_copy(ys_ref.at[_tile_rows(dest_ref[0, 0, 2 * t + k])],
                                  buf.at[k, _tile_rows(t)], sem).start()
        return carry

    lax.fori_loop(0, tb, issue, 0)
    for k in range(2):
        pltpu.make_async_copy(ys_ref.at[pl.ds(0, tb * TILE_ROWS)], buf.at[k], sem).wait()

    g = rg_ref[...]
    g1, g2 = g[:, 0:1], g[:, 1:2]
    ya = _load_token_tiles(buf.at[0], tb)
    yb = _load_token_tiles(buf.at[1], tb)
    y = jnp.concatenate([g1 * a + g2 * b for a, b in zip(ya, yb)], axis=1)
    out_ref[...] = _rms(x_ref[...] + y, gain_ref[...])


def _combine(x2, ys, dest, rg, gain, tb):
    t, d = x2.shape
    return pl.pallas_call(
        _combine_kernel,
        grid=(t // tb,),
        in_specs=[
            pl.BlockSpec((1, 1, 2 * tb), lambda i: (i, 0, 0), memory_space=pltpu.SMEM),
            pl.BlockSpec((tb, d), lambda i: (i, 0)),
            pl.BlockSpec((tb, 8), lambda i: (i, 0)),
            pl.BlockSpec((1, d), lambda i: (0, 0)),
            pl.BlockSpec(memory_space=pl.ANY),
        ],
        out_specs=pl.BlockSpec((tb, d), lambda i: (i, 0)),
        out_shape=jax.ShapeDtypeStruct((t, d), F32),
        scratch_shapes=[pltpu.VMEM((2, tb * TILE_ROWS, LANES), F32), pltpu.SemaphoreType.DMA(())],
        compiler_params=_params(("arbitrary",)),
        name="moe_combine",
    )(dest.reshape(t // tb, 1, 2 * tb), x2, rg, gain.reshape(1, d), ys)


def _pad_heads(w, heads, width, padded):
    lead = w.shape[:-1]
    w = w.reshape(lead + (heads, width))
    w = jnp.pad(w, [(0, 0)] * len(lead) + [(0, 0), (0, padded - width)])
    return w.reshape(lead + (heads * padded,))


def _layer_a_weights(w_in, gate_w, gate_b, out_norm, w_out):
    dqk = GLA_HEADS * GLA_DK
    dv = GLA_HEADS * GLA_DV
    o = 0
    wq = w_in[:, o:o + dqk]; o += dqk
    wk = w_in[:, o:o + dqk]; o += dqk
    wv = w_in[:, o:o + dv]; o += dv
    wl = w_in[:, o:o + GLA_LOWRANK]; o += GLA_LOWRANK
    wr = w_in[:, o:o + dv]; o += dv
    wm = w_in[:, o:o + MEM_WIDTH]
    w = jnp.concatenate([
        _pad_heads(wq, GLA_HEADS, GLA_DK, GLA_DK_PAD),
        _pad_heads(wk, GLA_HEADS, GLA_DK, GLA_DK_PAD),
        _pad_heads(wv, GLA_HEADS, GLA_DV, GLA_DV_PAD),
        _pad_heads(wr, GLA_HEADS, GLA_DV, GLA_DV_PAD),
        wm,
        jnp.pad(wl, ((0, 0), (0, LANES - GLA_LOWRANK))),
    ], axis=1).astype(BF16)
    gw = _pad_heads(gate_w, GLA_HEADS, GLA_DK, GLA_DK_PAD)
    gw = jnp.pad(gw, ((0, LANES - GLA_LOWRANK), (0, 0))).astype(BF16)
    gb = _pad_heads(gate_b, GLA_HEADS, GLA_DK, GLA_DK_PAD).reshape(1, -1)
    og = jnp.pad(out_norm, (0, GLA_DV_PAD - GLA_DV)).reshape(1, -1)
    wo_main = w_out[:dv].reshape(GLA_HEADS, GLA_DV, -1)
    wo_main = jnp.pad(wo_main, ((0, 0), (0, GLA_DV_PAD - GLA_DV), (0, 0))).reshape(GLA_HEADS * GLA_DV_PAD, -1)
    wo = jnp.concatenate([wo_main, w_out[dv:]], axis=0).astype(BF16)
    return w, gw, gb, og, wo


def kernel(x, mem, positions, mix_norm, w_in_a, gla_gate_w, gla_gate_b, gla_out_norm, w_in_b, swa_sinks,
           mem_norm, w_mem_kv, w_out, ffn_norm, ffn_w_gate_up, ffn_w_down, router_w, exp_w_gate_up,
           exp_w_down, final_norm):
    b, s, d = x.shape
    t = b * s
    x2 = x.reshape(t, d)
    assert d == TILE_ROWS * LANES
    tq = min(512, s)
    tm_moe = 512
    gb = min(1024, t)

    km, vm = _memkv(mem, mem_norm, w_mem_kv)

    wa, gw, gbias, og, wo_a = _layer_a_weights(w_in_a[0], gla_gate_w[0], gla_gate_b[0], gla_out_norm[0], w_out[0])
    z = _proj(x2, mix_norm[0], wa, tq)
    x2 = _gla_mix(z, x2, gw, gbias, og, km[0], vm[0], wo_a, b, s, tq)
    x2 = _ffn(x2, ffn_norm[0], ffn_w_gate_up[0].astype(BF16), ffn_w_down[0].astype(BF16), tq, 1408)

    n_q = SWA_HEADS * HEAD_DIM
    n_kv = SWA_KV_HEADS * HEAD_DIM
    wb = jnp.concatenate([w_in_b[0][:, :n_q] * (HEAD_DIM ** -0.5), w_in_b[0][:, n_q:]], axis=1).astype(BF16)
    half = ROPE_DIM // 2
    inv_freq = jnp.power(jnp.float32(ROPE_THETA), -jnp.arange(half, dtype=F32) / half)
    invf = jnp.tile(inv_freq, LANES // half).reshape(1, LANES)
    pos = positions.astype(F32).reshape(t, 1)
    z = _proj(x2, mix_norm[1], wb, tq, rope=(pos, invf, n_q + n_kv))
    x2 = _swa_mix(z, x2, swa_sinks[0], km[1], vm[1], w_out[1].astype(BF16), b, s, tq)

    rw = jnp.pad(router_w[0], ((0, 0), (0, LANES - N_EXPERTS)))
    h, ri, rg, counts = _router(x2, ffn_norm[1], rw, tq)

    cnt = counts[0, :N_EXPERTS]
    padded = ((cnt + tm_moe - 1) // tm_moe) * tm_moe
    ends = jnp.cumsum(padded)
    offs = ends - padded
    n_tiles = 2 * t // tm_moe + N_EXPERTS
    n_used = (ends[-1] // tm_moe).astype(jnp.int32)
    tile_start = jnp.arange(n_tiles, dtype=jnp.int32) * tm_moe
    tile_expert = jnp.sum(tile_start[:, None] >= ends[None, :], axis=1).astype(jnp.int32)
    tile_expert = jnp.minimum(tile_expert, N_EXPERTS - 1)
    last_expert = tile_expert[jnp.maximum(n_used - 1, 0)]
    tile_expert = jnp.where(jnp.arange(n_tiles) < n_used, tile_expert, last_expert)
    experts = ri[:, 0:2]
    off_of = jnp.sum(jnp.where(experts[:, :, None] == jnp.arange(N_EXPERTS)[None, None, :],
                               offs[None, None, :], 0), axis=-1)
    dest = (off_of + ri[:, 2:4]).astype(jnp.int32)

    pad_hi = ends.at[N_EXPERTS - 1].set(n_tiles * tm_moe)
    xs = _dispatch(h, dest, (offs + cnt).astype(jnp.int32), pad_hi.astype(jnp.int32), n_tiles * tm_moe, gb)
    ys = _moe(xs, tile_expert, n_used.reshape(1), exp_w_gate_up[0].astype(BF16), exp_w_down[0].astype(BF16),
              tm_moe, 1792)
    out = _combine(x2, ys, dest, rg, final_norm, gb)
    return out.reshape(b, s, d)
```

```python
import functools

import jax
import jax.numpy as jnp
from jax import lax
from jax.experimental import pallas as pl
from jax.experimental.pallas import tpu as pltpu

F32 = jnp.float32
BF16 = jnp.bfloat16

HEAD_DIM = 64
MEM_HEADS = 4
MEM_WIDTH = MEM_HEADS * HEAD_DIM
GLA_HEADS = 4
GLA_DK = 96
GLA_DV = 192
GLA_LOWRANK = 16
GLA_TAU = 16.0
GLA_CHUNK = 64
SWA_HEADS = 12
SWA_KV_HEADS = 4
WINDOW = 128
ROPE_DIM = 16
ROPE_THETA = 500000.0
N_EXPERTS = 8
EPS = 1e-6

LANES = 128
TILE_ROWS = 8
GLA_DK_PAD = 128
GLA_DV_PAD = 256
VMEM_LIMIT = 56 * 1024 * 1024

NT_DIMS = (((1,), (1,)), ((), ()))
TN_DIMS = (((0,), (0,)), ((), ()))


def _params(sem):
    return pltpu.CompilerParams(dimension_semantics=sem, vmem_limit_bytes=VMEM_LIMIT)


def _rms(xf, gain):
    return xf * lax.rsqrt(jnp.mean(xf * xf, axis=-1, keepdims=True) + EPS) * gain


def _dot(a, b):
    return jnp.dot(a, b, preferred_element_type=F32)


def _silu(x):
    return x * jax.nn.sigmoid(x)


def _store_token_tiles(ref, val):
    n = val.shape[0]
    for c in range(TILE_ROWS):
        ref[pl.ds(c, n, stride=TILE_ROWS), :] = val[:, c * LANES:(c + 1) * LANES].astype(ref.dtype)


def _load_token_tiles(ref, n):
    return [ref[pl.ds(c, n, stride=TILE_ROWS), :] for c in range(TILE_ROWS)]


def _memkv_kernel(mem_ref, gain_ref, w_ref, km_ref, vm_ref):
    mem_n = _rms(mem_ref[0], gain_ref[0]).astype(BF16)
    kv = _dot(mem_n, w_ref[0].astype(BF16))
    k = kv[:, :MEM_WIDTH] * (HEAD_DIM ** -0.5)
    v = kv[:, MEM_WIDTH:]
    head = lax.broadcasted_iota(jnp.int32, k.shape, 1) // HEAD_DIM
    for h in range(MEM_HEADS):
        km_ref[0, 0, h] = jnp.where(head == h, k, 0.0).astype(BF16)
        vm_ref[0, 0, h] = jnp.where(head == h, v, 0.0).astype(BF16)


def _memkv(mem, mem_norm, w_mem_kv):
    depth = mem_norm.shape[0]
    b, n_mem, d = mem.shape
    out = jax.ShapeDtypeStruct((depth, b, MEM_HEADS, n_mem, MEM_WIDTH), BF16)
    blk = pl.BlockSpec((1, 1, MEM_HEADS, n_mem, MEM_WIDTH), lambda i, bb: (i, bb, 0, 0, 0))
    return pl.pallas_call(
        _memkv_kernel,
        grid=(depth, b),
        in_specs=[
            pl.BlockSpec((1, n_mem, d), lambda i, bb: (bb, 0, 0)),
            pl.BlockSpec((1, 1, d), lambda i, bb: (i, 0, 0)),
            pl.BlockSpec((1, d, 2 * MEM_WIDTH), lambda i, bb: (i, 0, 0)),
        ],
        out_specs=[blk, blk],
        out_shape=[out, out],
        compiler_params=_params(("arbitrary", "arbitrary")),
        name="mem_kv",
    )(mem, mem_norm.reshape(depth, 1, d), w_mem_kv)


def _mem_attention(qm, km_ref, vm_ref):
    acc = jnp.zeros((qm.shape[0], MEM_WIDTH), F32)
    for h in range(MEM_HEADS):
        s = lax.dot_general(qm, km_ref[h], NT_DIMS, preferred_element_type=F32)
        p = jnp.exp(s - jnp.max(s, axis=-1, keepdims=True))
        p = p / jnp.sum(p, axis=-1, keepdims=True)
        acc = acc + _dot(p.astype(BF16), vm_ref[h])
    return acc


def _proj_kernel(x_ref, gain_ref, w_ref, z_ref, *, col_chunk):
    h = _rms(x_ref[...], gain_ref[...]).astype(BF16)
    n = w_ref.shape[1]
    for c0 in range(0, n, col_chunk):
        z_ref[:, c0:c0 + col_chunk] = _dot(h, w_ref[:, c0:c0 + col_chunk]).astype(BF16)


def _proj_rope_kernel(x_ref, gain_ref, w_ref, pos_ref, invf_ref, z_ref, *, rope_cols):
    h = _rms(x_ref[...], gain_ref[...]).astype(BF16)
    ang = pos_ref[...] * invf_ref[...]
    lane = lax.broadcasted_iota(jnp.int32, ang.shape, 1) % HEAD_DIM
    half = ROPE_DIM // 2
    cosf = jnp.where(lane < ROPE_DIM, jnp.cos(ang), 1.0)
    sn = jnp.sin(ang)
    sinf = jnp.where(lane < half, -sn, jnp.where(lane < ROPE_DIM, sn, 0.0))
    first = lane < half
    n = w_ref.shape[1]
    for c0 in range(0, n, LANES):
        zc = _dot(h, w_ref[:, c0:c0 + LANES])
        if c0 < rope_cols:
            partner = jnp.where(first, pltpu.roll(zc, LANES - half, 1), pltpu.roll(zc, half, 1))
            zc = zc * cosf + partner * sinf
        z_ref[:, c0:c0 + LANES] = zc.astype(BF16)


def _proj(x2, gain, w, tm, rope=None):
    t, d = x2.shape
    n = w.shape[1]
    in_specs = [
        pl.BlockSpec((tm, d), lambda i: (i, 0)),
        pl.BlockSpec((1, d), lambda i: (0, 0)),
        pl.BlockSpec((d, n), lambda i: (0, 0)),
    ]
    args = [x2, gain.reshape(1, d), w]
    if rope is None:
        kern = functools.partial(_proj_kernel, col_chunk=n // 3)
    else:
        pos, invf, rope_cols = rope
        kern = functools.partial(_proj_rope_kernel, rope_cols=rope_cols)
        in_specs += [pl.BlockSpec((tm, 1), lambda i: (i, 0)), pl.BlockSpec((1, LANES), lambda i: (0, 0))]
        args += [pos, invf]
    return pl.pallas_call(
        kern,
        grid=(t // tm,),
        in_specs=in_specs,
        out_specs=pl.BlockSpec((tm, n), lambda i: (i, 0)),
        out_shape=jax.ShapeDtypeStruct((t, n), BF16),
        compiler_params=_params(("parallel",)),
        name="in_proj" if rope is None else "in_proj_rope",
    )(*args)


def _gla_mix_kernel(q_ref, k_ref, v_ref, r_ref, qm_ref, gl_ref, x_ref, gw_ref, gb_ref, on_ref, tri_ref, blk_ref,
                    km_ref, vm_ref, wo_ref, out_ref, qd_s, kd_s, ke_s, dec_s, o_s, sc_s, main_s, st_s):
    @pl.when(pl.program_id(1) == 0)
    def _():
        st_s[...] = jnp.zeros_like(st_s)

    tq = q_ref.shape[0]
    c_len = GLA_CHUNK
    n_chunks = tq // c_len
    logits = _dot(gl_ref[...], gw_ref[...]) + gb_ref[...]
    log_sig = jnp.minimum(logits, 0.0) - jnp.log(1.0 + jnp.exp(-jnp.abs(logits)))
    lane = lax.broadcasted_iota(jnp.int32, logits.shape, 1) % GLA_DK_PAD
    la = jnp.where(lane < GLA_DK, log_sig * (1.0 / GLA_TAU), 0.0)

    la_hi = la.astype(BF16)
    la_lo = (la - la_hi.astype(F32)).astype(BF16)
    tri = tri_ref[...]
    blk = blk_ref[...]
    b = _dot(tri, la_hi) + _dot(tri, la_lo)
    b_last = _dot(blk, la_hi) + _dot(blk, la_lo)
    qf = q_ref[...].astype(F32)
    kf = k_ref[...].astype(F32)
    qd_s[...] = (qf * (GLA_DK ** -0.5) * jnp.exp(b)).astype(BF16)
    kd_s[...] = (kf * jnp.exp(-b)).astype(BF16)
    ke_s[...] = (kf * jnp.exp(b_last - b)).astype(BF16)
    dec_s[...] = jnp.exp(b_last)

    kcols = [slice(h * GLA_DK_PAD, (h + 1) * GLA_DK_PAD) for h in range(GLA_HEADS)]
    vcols = [slice(h * GLA_DV_PAD, (h + 1) * GLA_DV_PAD) for h in range(GLA_HEADS)]

    sb = min(4 * c_len, tq)
    ri = lax.broadcasted_iota(jnp.int32, (sb, sb), 0)
    ci = lax.broadcasted_iota(jnp.int32, (sb, sb), 1)
    intra = jnp.logical_and(ri // c_len == ci // c_len, ci <= ri)
    for s0 in range(0, tq, sb):
        rows = slice(s0, s0 + sb)
        for h in range(GLA_HEADS):
            attn = lax.dot_general(qd_s[rows, kcols[h]], kd_s[rows, kcols[h]], NT_DIMS,
                                   preferred_element_type=F32)
            attn = jnp.where(intra, attn, 0.0).astype(BF16)
            o_s[rows, vcols[h]] = _dot(attn, v_ref[rows, vcols[h]])

    for c in range(n_chunks):
        rows = slice(c * c_len, (c + 1) * c_len)
        for h in range(GLA_HEADS):
            st = st_s[h]
            sc_s[c * GLA_HEADS + h] = st.astype(BF16)
            kv = lax.dot_general(v_ref[rows, vcols[h]], ke_s[rows, kcols[h]], TN_DIMS, preferred_element_type=F32)
            st_s[h] = st * dec_s[c * c_len:c * c_len + 1, kcols[h]] + kv
    for c in range(n_chunks):
        rows = slice(c * c_len, (c + 1) * c_len)
        for h in range(GLA_HEADS):
            o_s[rows, vcols[h]] += lax.dot_general(qd_s[rows, kcols[h]], sc_s[c * GLA_HEADS + h], NT_DIMS,
                                                   preferred_element_type=F32)

    for h in range(GLA_HEADS):
        o = o_s[:, vcols[h]]
        ms = jnp.sum(o * o, axis=-1, keepdims=True) * (1.0 / GLA_DV)
        o_n = o * lax.rsqrt(ms + EPS) * on_ref[...]
        main_s[:, vcols[h]] = (o_n * _silu(r_ref[:, vcols[h]].astype(F32))).astype(BF16)

    mem_out = _mem_attention(qm_ref[...], km_ref, vm_ref)
    n_main = GLA_HEADS * GLA_DV_PAD
    mixed = _dot(main_s[...], wo_ref[0:n_main, :]) + _dot(mem_out.astype(BF16), wo_ref[n_main:, :])
    out_ref[...] = x_ref[...] + mixed


def _gla_mix(z, x2, gate_w, gate_b, out_gain, km, vm, wo, b, s, tq):
    t, d = x2.shape
    nj = s // tq
    wq = GLA_HEADS * GLA_DK_PAD
    wv = GLA_HEADS * GLA_DV_PAD
    row = lambda bb, j: bb * nj + j
    qm_blk = (2 * wq + 2 * wv) // MEM_WIDTH
    gl_blk = (2 * wq + 2 * wv + MEM_WIDTH) // LANES
    n_mem = km.shape[2]
    in_specs = [
        pl.BlockSpec((tq, wq), lambda bb, j: (row(bb, j), 0)),
        pl.BlockSpec((tq, wq), lambda bb, j: (row(bb, j), 1)),
        pl.BlockSpec((tq, wv), lambda bb, j: (row(bb, j), 1)),
        pl.BlockSpec((tq, wv), lambda bb, j: (row(bb, j), 2)),
        pl.BlockSpec((tq, MEM_WIDTH), lambda bb, j: (row(bb, j), qm_blk)),
        pl.BlockSpec((tq, LANES), lambda bb, j: (row(bb, j), gl_blk)),
        pl.BlockSpec((tq, d), lambda bb, j: (row(bb, j), 0)),
        pl.BlockSpec(gate_w.shape, lambda bb, j: (0, 0)),
        pl.BlockSpec(gate_b.shape, lambda bb, j: (0, 0)),
        pl.BlockSpec(out_gain.shape, lambda bb, j: (0, 0)),
        pl.BlockSpec((tq, tq), lambda bb, j: (0, 0)),
        pl.BlockSpec((tq, tq), lambda bb, j: (0, 0)),
        pl.BlockSpec((None, MEM_HEADS, n_mem, MEM_WIDTH), lambda bb, j: (bb, 0, 0, 0)),
        pl.BlockSpec((None, MEM_HEADS, n_mem, MEM_WIDTH), lambda bb, j: (bb, 0, 0, 0)),
        pl.BlockSpec(wo.shape, lambda bb, j: (0, 0)),
    ]
    idx = jnp.arange(tq, dtype=jnp.int32)
    same = idx[:, None] // GLA_CHUNK == idx[None, :] // GLA_CHUNK
    tri = jnp.logical_and(same, idx[None, :] <= idx[:, None]).astype(BF16)
    blk = same.astype(BF16)
    n_chunks = tq // GLA_CHUNK
    return pl.pallas_call(
        _gla_mix_kernel,
        grid=(b, nj),
        in_specs=in_specs,
        out_specs=pl.BlockSpec((tq, d), lambda bb, j: (row(bb, j), 0)),
        out_shape=jax.ShapeDtypeStruct((t, d), F32),
        scratch_shapes=[
            pltpu.VMEM((tq, wq), BF16),
            pltpu.VMEM((tq, wq), BF16),
            pltpu.VMEM((tq, wq), BF16),
            pltpu.VMEM((tq, wq), F32),
            pltpu.VMEM((tq, wv), F32),
            pltpu.VMEM((n_chunks * GLA_HEADS, GLA_DV_PAD, GLA_DK_PAD), BF16),
            pltpu.VMEM((tq, wv), BF16),
            pltpu.VMEM((GLA_HEADS, GLA_DV_PAD, GLA_DK_PAD), F32),
        ],
        compiler_params=_params(("arbitrary", "arbitrary")),
        name="gla_mix",
    )(z, z, z, z, z, z, x2, gate_w, gate_b, out_gain, tri, blk, km, vm, wo)


def _swa_mix_kernel(sink_ref, q_ref, k_ref, v_ref, kp_ref, vp_ref, qm_ref, x_ref, km_ref, vm_ref,
                    wo_ref, out_ref, qt_s, kg_s, vt_s, s_s, p_s, maint_s):
    has_prev = pl.program_id(1) > 0
    tq = q_ref.shape[0]
    w = WINDOW
    group = SWA_HEADS // SWA_KV_HEADS
    qt_s[...] = jnp.transpose(q_ref[...].astype(F32)).astype(BF16)
    for g in range(SWA_KV_HEADS):
        gc = slice(g * HEAD_DIM, (g + 1) * HEAD_DIM)
        kg_s[g, 0:w, :] = kp_ref[:, gc]
        kg_s[g, w:, :] = k_ref[:, gc]
    vt_s[:, 0:w] = jnp.transpose(vp_ref[...].astype(F32)).astype(BF16)
    vt_s[:, w:] = jnp.transpose(v_ref[...].astype(F32)).astype(BF16)
    kj = lax.broadcasted_iota(jnp.int32, (2 * w, w), 0)
    qi = lax.broadcasted_iota(jnp.int32, (2 * w, w), 1)
    band = jnp.logical_and(kj > qi, kj <= qi + w)
    bias = jnp.where(band, 0.0, -jnp.inf)
    bias_first = jnp.where(jnp.logical_and(band, jnp.logical_or(kj >= w, has_prev)), 0.0, -jnp.inf)
    sink = sink_ref[...]
    for sub in range(tq // w):
        r0 = sub * w
        sub_bias = bias_first if sub == 0 else bias
        for h in range(SWA_HEADS):
            s_s[h] = _dot(kg_s[h // group, r0:r0 + 2 * w, :],
                          qt_s[h * HEAD_DIM:(h + 1) * HEAD_DIM, r0:r0 + w]) + sub_bias
        s = s_s[...]
        m = jnp.maximum(jnp.max(s, axis=1, keepdims=True), sink)
        p = jnp.exp(s - m)
        inv = 1.0 / (jnp.sum(p, axis=1, keepdims=True) + jnp.exp(sink - m))
        p_s[...] = p.astype(BF16)
        for h in range(SWA_HEADS):
            g = h // group
            o_t = _dot(vt_s[g * HEAD_DIM:(g + 1) * HEAD_DIM, r0:r0 + 2 * w], p_s[h])
            maint_s[h * HEAD_DIM:(h + 1) * HEAD_DIM, r0:r0 + w] = (o_t * inv[h]).astype(BF16)

    mem_out = _mem_attention(qm_ref[...], km_ref, vm_ref)
    n_main = SWA_HEADS * HEAD_DIM
    mixed = (lax.dot_general(maint_s[...], wo_ref[0:n_main, :], TN_DIMS, preferred_element_type=F32)
             + _dot(mem_out.astype(BF16), wo_ref[n_main:, :]))
    out_ref[...] = x_ref[...] + mixed


def _swa_mix(z, x2, sinks, km, vm, wo, b, s, tq):
    t, d = x2.shape
    nj = s // tq
    wq = SWA_HEADS * HEAD_DIM
    wkv = SWA_KV_HEADS * HEAD_DIM
    row = lambda bb, j: bb * nj + j
    per = tq // WINDOW
    prev = lambda bb, j: jnp.maximum(row(bb, j) * per - 1, 0)
    k_blk = wq // wkv
    n_mem = km.shape[2]
    sink_col = jnp.broadcast_to(sinks.astype(F32)[:, None, None], (SWA_HEADS, 1, WINDOW))
    in_specs = [
        pl.BlockSpec(sink_col.shape, lambda bb, j: (0, 0, 0)),
        pl.BlockSpec((tq, wq), lambda bb, j: (row(bb, j), 0)),
        pl.BlockSpec((tq, wkv), lambda bb, j: (row(bb, j), k_blk)),
        pl.BlockSpec((tq, wkv), lambda bb, j: (row(bb, j), k_blk + 1)),
        pl.BlockSpec((WINDOW, wkv), lambda bb, j: (prev(bb, j), k_blk)),
        pl.BlockSpec((WINDOW, wkv), lambda bb, j: (prev(bb, j), k_blk + 1)),
        pl.BlockSpec((tq, MEM_WIDTH), lambda bb, j: (row(bb, j), k_blk + 2)),
        pl.BlockSpec((tq, d), lambda bb, j: (row(bb, j), 0)),
        pl.BlockSpec((None, MEM_HEADS, n_mem, MEM_WIDTH), lambda bb, j: (bb, 0, 0, 0)),
        pl.BlockSpec((None, MEM_HEADS, n_mem, MEM_WIDTH), lambda bb, j: (bb, 0, 0, 0)),
        pl.BlockSpec(wo.shape, lambda bb, j: (0, 0)),
    ]
    return pl.pallas_call(
        _swa_mix_kernel,
        grid=(b, nj),
        in_specs=in_specs,
        out_specs=pl.BlockSpec((tq, d), lambda bb, j: (row(bb, j), 0)),
        out_shape=jax.ShapeDtypeStruct((t, d), F32),
        scratch_shapes=[
            pltpu.VMEM((wq, tq), BF16),
            pltpu.VMEM((SWA_KV_HEADS, tq + WINDOW, HEAD_DIM), BF16),
            pltpu.VMEM((wkv, tq + WINDOW), BF16),
            pltpu.VMEM((SWA_HEADS, 2 * WINDOW, WINDOW), F32),
            pltpu.VMEM((SWA_HEADS, 2 * WINDOW, WINDOW), BF16),
            pltpu.VMEM((wq, tq), BF16),
        ],
        compiler_params=_params(("parallel", "parallel")),
        name="swa_mix",
    )(sink_col, z, z, z, z, z, z, x2, km, vm, wo)


def _ffn_kernel(x_ref, gain_ref, wg_ref, wu_ref, wd_ref, out_ref, h_s, acc_s):
    c = pl.program_id(1)

    @pl.when(c == 0)
    def _():
        x = x_ref[...]
        h_s[...] = _rms(x, gain_ref[...]).astype(BF16)
        acc_s[...] = x

    h = h_s[...]
    a = (_silu(_dot(h, wg_ref[...])) * _dot(h, wu_ref[...])).astype(BF16)
    acc_s[...] += _dot(a, wd_ref[...])

    @pl.when(c == pl.num_programs(1) - 1)
    def _():
        out_ref[...] = acc_s[...]


def _ffn(x2, gain, w_gu, w_down, tm, ffc):
    t, d = x2.shape
    d_ff = w_down.shape[0]
    nc = d_ff // ffc
    return pl.pallas_call(
        _ffn_kernel,
        grid=(t // tm, nc),
        in_specs=[
            pl.BlockSpec((tm, d), lambda i, c: (i, 0)),
            pl.BlockSpec((1, d), lambda i, c: (0, 0)),
            pl.BlockSpec((d, ffc), lambda i, c: (0, c)),
            pl.BlockSpec((d, ffc), lambda i, c: (0, nc + c)),
            pl.BlockSpec((ffc, d), lambda i, c: (c, 0)),
        ],
        out_specs=pl.BlockSpec((tm, d), lambda i, c: (i, 0)),
        out_shape=jax.ShapeDtypeStruct((t, d), F32),
        scratch_shapes=[pltpu.VMEM((tm, d), BF16), pltpu.VMEM((tm, d), F32)],
        compiler_params=_params(("parallel", "arbitrary")),
        name="ffn_dense",
    )(x2, gain.reshape(1, d), w_gu, w_gu, w_down)


def _router_kernel(x_ref, gain_ref, rw_ref, h_ref, ri_ref, rg_ref, cnt_ref, carry_s):
    @pl.when(pl.program_id(0) == 0)
    def _():
        carry_s[...] = jnp.zeros_like(carry_s)

    tb = x_ref.shape[0]
    h = _rms(x_ref[...], gain_ref[...])
    _store_token_tiles(h_ref, h)
    logits = jnp.dot(h, rw_ref[...], precision=lax.Precision.HIGHEST, preferred_element_type=F32)
    lane = lax.broadcasted_iota(jnp.int32, logits.shape, 1)
    logits = jnp.where(lane < N_EXPERTS, logits, -jnp.inf)
    m1 = jnp.max(logits, axis=-1, keepdims=True)
    i1 = jnp.min(jnp.where(logits == m1, lane, LANES), axis=-1, keepdims=True)
    rest = jnp.where(lane == i1, -jnp.inf, logits)
    m2 = jnp.max(rest, axis=-1, keepdims=True)
    i2 = jnp.min(jnp.where(rest == m2, lane, LANES), axis=-1, keepdims=True)
    e2 = jnp.exp(m2 - m1)
    g1 = 1.0 / (1.0 + e2)
    g2 = e2 / (1.0 + e2)

    sel1 = lane == i1
    sel2 = lane == i2
    onehot = jnp.where(jnp.logical_or(sel1, sel2), 1.0, 0.0)
    ti = lax.broadcasted_iota(jnp.int32, (tb, tb), 0)
    tj = lax.broadcasted_iota(jnp.int32, (tb, tb), 1)
    before = jnp.where(tj < ti, 1.0, 0.0).astype(BF16)
    seen = carry_s[...] + _dot(before, onehot.astype(BF16))
    rank1 = jnp.sum(jnp.where(sel1, seen, 0.0), axis=-1, keepdims=True).astype(jnp.int32)
    rank2 = jnp.sum(jnp.where(sel2, seen, 0.0), axis=-1, keepdims=True).astype(jnp.int32)
    carry_s[...] += jnp.sum(onehot, axis=0, keepdims=True)
    cnt_ref[...] = carry_s[...].astype(jnp.int32)

    ri = jnp.where(lane == 0, i1, jnp.where(lane == 1, i2, jnp.where(lane == 2, rank1, rank2)))
    rg = jnp.where(lane == 0, g1, g2)
    ri_ref[...] = ri[:, :8]
    rg_ref[...] = rg[:, :8]


def _router(x2, gain, rw, tb):
    t, d = x2.shape
    return pl.pallas_call(
        _router_kernel,
        grid=(t // tb,),
        in_specs=[
            pl.BlockSpec((tb, d), lambda i: (i, 0)),
            pl.BlockSpec((1, d), lambda i: (0, 0)),
            pl.BlockSpec((d, LANES), lambda i: (0, 0)),
        ],
        out_specs=[
            pl.BlockSpec((tb * TILE_ROWS, LANES), lambda i: (i, 0)),
            pl.BlockSpec((tb, 8), lambda i: (i, 0)),
            pl.BlockSpec((tb, 8), lambda i: (i, 0)),
            pl.BlockSpec((1, LANES), lambda i: (0, 0)),
        ],
        out_shape=[
            jax.ShapeDtypeStruct((t * TILE_ROWS, LANES), F32),
            jax.ShapeDtypeStruct((t, 8), jnp.int32),
            jax.ShapeDtypeStruct((t, 8), F32),
            jax.ShapeDtypeStruct((1, LANES), jnp.int32),
        ],
        scratch_shapes=[pltpu.VMEM((1, LANES), F32)],
        compiler_params=_params(("arbitrary",)),
        name="router",
    )(x2, gain.reshape(1, d), rw)


def _tile_rows(i):
    return pl.ds(pl.multiple_of(i * TILE_ROWS, TILE_ROWS), TILE_ROWS)


def _dispatch_kernel(pad_lo_ref, pad_hi_ref, dest_ref, src_ref, dst_ref, sem):
    gb = dest_ref.shape[2] // 2

    def issue(t, carry):
        tile = src_ref.at[_tile_rows(t)]
        pltpu.make_async_copy(tile, dst_ref.at[_tile_rows(dest_ref[0, 0, 2 * t])], sem).start()
        pltpu.make_async_copy(tile, dst_ref.at[_tile_rows(dest_ref[0, 0, 2 * t + 1])], sem).start()
        return carry

    lax.fori_loop(0, gb, issue, 0)
    for _ in range(2):
        pltpu.make_async_copy(src_ref, dst_ref.at[pl.ds(0, gb * TILE_ROWS)], sem).wait()

    @pl.when(pl.program_id(0) == pl.num_programs(0) - 1)
    def _():
        for e in range(N_EXPERTS):
            lo, hi = pad_lo_ref[e], pad_hi_ref[e]

            def fill(r, carry):
                pltpu.make_async_copy(src_ref.at[_tile_rows(0)], dst_ref.at[_tile_rows(r)], sem).start()
                return carry

            def drain(r, carry):
                pltpu.make_async_copy(src_ref.at[_tile_rows(0)], dst_ref.at[_tile_rows(r)], sem).wait()
                return carry

            lax.fori_loop(lo, hi, fill, 0)
            lax.fori_loop(lo, hi, drain, 0)


def _dispatch(src, dest, pad_lo, pad_hi, n_rows, gb):
    t = src.shape[0] // TILE_ROWS
    return pl.pallas_call(
        _dispatch_kernel,
        grid_spec=pltpu.PrefetchScalarGridSpec(
            num_scalar_prefetch=2,
            grid=(t // gb,),
            in_specs=[
                pl.BlockSpec((1, 1, 2 * gb), lambda i, lo, hi: (i, 0, 0), memory_space=pltpu.SMEM),
                pl.BlockSpec((gb * TILE_ROWS, LANES), lambda i, lo, hi: (i, 0)),
            ],
            out_specs=pl.BlockSpec(memory_space=pl.ANY),
            scratch_shapes=[pltpu.SemaphoreType.DMA(())],
        ),
        out_shape=jax.ShapeDtypeStruct((n_rows * TILE_ROWS, LANES), src.dtype),
        compiler_params=_params(("arbitrary",)),
        name="moe_dispatch",
    )(pad_lo, pad_hi, dest.reshape(t // gb, 1, 2 * gb), src)


def _moe_kernel(te_ref, nu_ref, xs_ref, wg_ref, wu_ref, wd_ref, out_ref, h_s, acc_s):
    i = pl.program_id(0)
    c = pl.program_id(1)
    last = pl.num_programs(1) - 1
    used = i < nu_ref[0]

    @pl.when(used)
    def _():
        tm = h_s.shape[0]

        @pl.when(c == 0)
        def _():
            for cc, part in enumerate(_load_token_tiles(xs_ref, tm)):
                h_s[:, cc * LANES:(cc + 1) * LANES] = part.astype(BF16)
            acc_s[...] = jnp.zeros_like(acc_s)

        h = h_s[...]
        a = (_silu(_dot(h, wg_ref[0])) * _dot(h, wu_ref[0])).astype(BF16)
        acc_s[...] += _dot(a, wd_ref[0])

        @pl.when(c == last)
        def _():
            _store_token_tiles(out_ref, acc_s[...])

    @pl.when(jnp.logical_and(jnp.logical_not(used), c == last))
    def _():
        out_ref[...] = jnp.zeros_like(out_ref)


def _moe(xs, tile_expert, n_used, w_gu, w_down, tm, ffc):
    n_rows = xs.shape[0] // TILE_ROWS
    d = w_down.shape[2]
    d_ff = w_down.shape[1]
    nc = d_ff // ffc
    nt = n_rows // tm

    def chunk(i, c, nu):
        return jnp.where(i < nu[0], c, nc - 1)

    return pl.pallas_call(
        _moe_kernel,
        grid_spec=pltpu.PrefetchScalarGridSpec(
            num_scalar_prefetch=2,
            grid=(nt, nc),
            in_specs=[
                pl.BlockSpec((tm * TILE_ROWS, LANES), lambda i, c, te, nu: (jnp.minimum(i, nu[0] - 1), 0)),
                pl.BlockSpec((1, d, ffc), lambda i, c, te, nu: (te[i], 0, chunk(i, c, nu))),
                pl.BlockSpec((1, d, ffc), lambda i, c, te, nu: (te[i], 0, nc + chunk(i, c, nu))),
                pl.BlockSpec((1, ffc, d), lambda i, c, te, nu: (te[i], chunk(i, c, nu), 0)),
            ],
            out_specs=pl.BlockSpec((tm * TILE_ROWS, LANES), lambda i, c, te, nu: (i, 0)),
            scratch_shapes=[pltpu.VMEM((tm, d), BF16), pltpu.VMEM((tm, d), F32)],
        ),
        out_shape=jax.ShapeDtypeStruct((n_rows * TILE_ROWS, LANES), F32),
        compiler_params=_params(("arbitrary", "arbitrary")),
        name="moe_experts",
    )(tile_expert, n_used, xs, w_gu, w_gu, w_down)


def _combine_kernel(dest_ref, x_ref, rg_ref, gain_ref, ys_ref, out_ref, buf, sem):
    tb = x_ref.shape[0]

    def issue(t, carry):
        for k in range(2):
            pltpu.make_async_copy(ys_ref.at[_tile_rows(dest_ref[0, 0, 2 * t + k])],
                                  buf.at[k, _tile_rows(t)], sem).start()
        return carry

    lax.fori_loop(0, tb, issue, 0)
    for k in range(2):
        pltpu.make_async_copy(ys_ref.at[pl.ds(0, tb * TILE_ROWS)], buf.at[k], sem).wait()

    g = rg_ref[...]
    g1, g2 = g[:, 0:1], g[:, 1:2]
    ya = _load_token_tiles(buf.at[0], tb)
    yb = _load_token_tiles(buf.at[1], tb)
    y = jnp.concatenate([g1 * a + g2 * b for a, b in zip(ya, yb)], axis=1)
    out_ref[...] = _rms(x_ref[...] + y, gain_ref[...])


def _combine(x2, ys, dest, rg, gain, tb):
    t, d = x2.shape
    return pl.pallas_call(
        _combine_kernel,
        grid=(t // tb,),
        in_specs=[
            pl.BlockSpec((1, 1, 2 * tb), lambda i: (i, 0, 0), memory_space=pltpu.SMEM),
            pl.BlockSpec((tb, d), lambda i: (i, 0)),
            pl.BlockSpec((tb, 8), lambda i: (i, 0)),
            pl.BlockSpec((1, d), lambda i: (0, 0)),
            pl.BlockSpec(memory_space=pl.ANY),
        ],
        out_specs=pl.BlockSpec((tb, d), lambda i: (i, 0)),
        out_shape=jax.ShapeDtypeStruct((t, d), F32),
        scratch_shapes=[pltpu.VMEM((2, tb * TILE_ROWS, LANES), F32), pltpu.SemaphoreType.DMA(())],
        compiler_params=_params(("arbitrary",)),
        name="moe_combine",
    )(dest.reshape(t // tb, 1, 2 * tb), x2, rg, gain.reshape(1, d), ys)


def _pad_heads(w, heads, width, padded):
    lead = w.shape[:-1]
    w = w.reshape(lead + (heads, width))
    w = jnp.pad(w, [(0, 0)] * len(lead) + [(0, 0), (0, padded - width)])
    return w.reshape(lead + (heads * padded,))


def _layer_a_weights(w_in, gate_w, gate_b, out_norm, w_out):
    dqk = GLA_HEADS * GLA_DK
    dv = GLA_HEADS * GLA_DV
    o = 0
    wq = w_in[:, o:o + dqk]; o += dqk
    wk = w_in[:, o:o + dqk]; o += dqk
    wv = w_in[:, o:o + dv]; o += dv
    wl = w_in[:, o:o + GLA_LOWRANK]; o += GLA_LOWRANK
    wr = w_in[:, o:o + dv]; o += dv
    wm = w_in[:, o:o + MEM_WIDTH]
    w = jnp.concatenate([
        _pad_heads(wq, GLA_HEADS, GLA_DK, GLA_DK_PAD),
        _pad_heads(wk, GLA_HEADS, GLA_DK, GLA_DK_PAD),
        _pad_heads(wv, GLA_HEADS, GLA_DV, GLA_DV_PAD),
        _pad_heads(wr, GLA_HEADS, GLA_DV, GLA_DV_PAD),
        wm,
        jnp.pad(wl, ((0, 0), (0, LANES - GLA_LOWRANK))),
    ], axis=1).astype(BF16)
    gw = _pad_heads(gate_w, GLA_HEADS, GLA_DK, GLA_DK_PAD)
    gw = jnp.pad(gw, ((0, LANES - GLA_LOWRANK), (0, 0))).astype(BF16)
    gb = _pad_heads(gate_b, GLA_HEADS, GLA_DK, GLA_DK_PAD).reshape(1, -1)
    og = jnp.pad(out_norm, (0, GLA_DV_PAD - GLA_DV)).reshape(1, -1)
    wo_main = w_out[:dv].reshape(GLA_HEADS, GLA_DV, -1)
    wo_main = jnp.pad(wo_main, ((0, 0), (0, GLA_DV_PAD - GLA_DV), (0, 0))).reshape(GLA_HEADS * GLA_DV_PAD, -1)
    wo = jnp.concatenate([wo_main, w_out[dv:]], axis=0).astype(BF16)
    return w, gw, gb, og, wo


def kernel(x, mem, positions, mix_norm, w_in_a, gla_gate_w, gla_gate_b, gla_out_norm, w_in_b, swa_sinks,
           mem_norm, w_mem_kv, w_out, ffn_norm, ffn_w_gate_up, ffn_w_down, router_w, exp_w_gate_up,
           exp_w_down, final_norm):
    b, s, d = x.shape
    t = b * s
    x2 = x.reshape(t, d)
    assert d == TILE_ROWS * LANES
    tq = min(512, s)
    tm_moe = 512
    gb = min(1024, t)

    km, vm = _memkv(mem, mem_norm, w_mem_kv)

    wa, gw, gbias, og, wo_a = _layer_a_weights(w_in_a[0], gla_gate_w[0], gla_gate_b[0], gla_out_norm[0], w_out[0])
    z = _proj(x2, mix_norm[0], wa, tq)
    x2 = _gla_mix(z, x2, gw, gbias, og, km[0], vm[0], wo_a, b, s, tq)
    x2 = _ffn(x2, ffn_norm[0], ffn_w_gate_up[0].astype(BF16), ffn_w_down[0].astype(BF16), tq, 1408)

    n_q = SWA_HEADS * HEAD_DIM
    n_kv = SWA_KV_HEADS * HEAD_DIM
    wb = jnp.concatenate([w_in_b[0][:, :n_q] * (HEAD_DIM ** -0.5), w_in_b[0][:, n_q:]], axis=1).astype(BF16)
    half = ROPE_DIM // 2
    inv_freq = jnp.power(jnp.float32(ROPE_THETA), -jnp.arange(half, dtype=F32) / half)
    invf = jnp.tile(inv_freq, LANES // half).reshape(1, LANES)
    pos = positions.astype(F32).reshape(t, 1)
    z = _proj(x2, mix_norm[1], wb, tq, rope=(pos, invf, n_q + n_kv))
    x2 = _swa_mix(z, x2, swa_sinks[0], km[1], vm[1], w_out[1].astype(BF16), b, s, tq)

    rw = jnp.pad(router_w[0], ((0, 0), (0, LANES - N_EXPERTS)))
    h, ri, rg, counts = _router(x2, ffn_norm[1], rw, tq)

    cnt = counts[0, :N_EXPERTS]
    padded = ((cnt + tm_moe - 1) // tm_moe) * tm_moe
    ends = jnp.cumsum(padded)
    offs = ends - padded
    n_tiles = 2 * t // tm_moe + N_EXPERTS
    n_used = (ends[-1] // tm_moe).astype(jnp.int32)
    tile_start = jnp.arange(n_tiles, dtype=jnp.int32) * tm_moe
    tile_expert = jnp.sum(tile_start[:, None] >= ends[None, :], axis=1).astype(jnp.int32)
    tile_expert = jnp.minimum(tile_expert, N_EXPERTS - 1)
    last_expert = tile_expert[jnp.maximum(n_used - 1, 0)]
    tile_expert = jnp.where(jnp.arange(n_tiles) < n_used, tile_expert, last_expert)
    experts = ri[:, 0:2]
    off_of = jnp.sum(jnp.where(experts[:, :, None] == jnp.arange(N_EXPERTS)[None, None, :],
                               offs[None, None, :], 0), axis=-1)
    dest = (off_of + ri[:, 2:4]).astype(jnp.int32)

    pad_hi = ends.at[N_EXPERTS - 1].set(n_tiles * tm_moe)
    xs = _dispatch(h, dest, (offs + cnt).astype(jnp.int32), pad_hi.astype(jnp.int32), n_tiles * tm_moe, gb)
    ys = _moe(xs, tile_expert, n_used.reshape(1), exp_w_gate_up[0].astype(BF16), exp_w_down[0].astype(BF16),
              tm_moe, 1792)
    out = _combine(x2, ys, dest, rg, final_norm, gb)
    return out.reshape(b, s, d)
```

```python
import functools

import jax
import jax.numpy as jnp
from jax import lax
from jax.experimental import pallas as pl
from jax.experimental.pallas import tpu as pltpu

F32 = jnp.float32
BF16 = jnp.bfloat16

HEAD_DIM = 64
MEM_HEADS = 4
MEM_WIDTH = MEM_HEADS * HEAD_DIM
GLA_HEADS = 4
GLA_DK = 96
GLA_DV = 192
GLA_LOWRANK = 16
GLA_TAU = 16.0
GLA_CHUNK = 64
SWA_HEADS = 12
SWA_KV_HEADS = 4
WINDOW = 128
ROPE_DIM = 16
ROPE_THETA = 500000.0
N_EXPERTS = 8
EPS = 1e-6

LANES = 128
TILE_ROWS = 8
GLA_DK_PAD = 128
GLA_DV_PAD = 256
VMEM_LIMIT = 56 * 1024 * 1024
ISSUE_UNROLL = 4
MXU_WIDTH = 256

NT_DIMS = (((1,), (1,)), ((), ()))
TN_DIMS = (((0,), (0,)), ((), ()))


def _params(sem):
    return pltpu.CompilerParams(dimension_semantics=sem, vmem_limit_bytes=VMEM_LIMIT)


def _rms(xf, gain):
    return xf * lax.rsqrt(jnp.mean(xf * xf, axis=-1, keepdims=True) + EPS) * gain


def _dot(a, b):
    return jnp.dot(a, b, preferred_element_type=F32)


def _silu(x):
    return x * jax.nn.sigmoid(x)


def _store_token_tiles(ref, val):
    n = val.shape[0]
    for c in range(TILE_ROWS):
        ref[pl.ds(c, n, stride=TILE_ROWS), :] = val[:, c * LANES:(c + 1) * LANES].astype(ref.dtype)


def _load_token_tiles(ref, n):
    return [ref[pl.ds(c, n, stride=TILE_ROWS), :] for c in range(TILE_ROWS)]


def _memkv_kernel(mem_ref, gain_ref, w_ref, km_ref, vm_ref):
    mem_n = _rms(mem_ref[0], gain_ref[0]).astype(BF16)
    kv = _dot(mem_n, w_ref[0].astype(BF16))
    k = kv[:, :MEM_WIDTH] * (HEAD_DIM ** -0.5)
    v = kv[:, MEM_WIDTH:]
    head = lax.broadcasted_iota(jnp.int32, k.shape, 1) // HEAD_DIM
    for h in range(MEM_HEADS):
        km_ref[0, 0, h] = jnp.where(head == h, k, 0.0).astype(BF16)
        vm_ref[0, 0, h] = jnp.where(head == h, v, 0.0).astype(BF16)


def _memkv(mem, mem_norm, w_mem_kv):
    depth = mem_norm.shape[0]
    b, n_mem, d = mem.shape
    out = jax.ShapeDtypeStruct((depth, b, MEM_HEADS, n_mem, MEM_WIDTH), BF16)
    blk = pl.BlockSpec((1, 1, MEM_HEADS, n_mem, MEM_WIDTH), lambda i, bb: (i, bb, 0, 0, 0))
    return pl.pallas_call(
        _memkv_kernel,
        grid=(depth, b),
        in_specs=[
            pl.BlockSpec((1, n_mem, d), lambda i, bb: (bb, 0, 0)),
            pl.BlockSpec((1, 1, d), lambda i, bb: (i, 0, 0)),
            pl.BlockSpec((1, d, 2 * MEM_WIDTH), lambda i, bb: (i, 0, 0)),
        ],
        out_specs=[blk, blk],
        out_shape=[out, out],
        compiler_params=_params(("arbitrary", "arbitrary")),
        name="mem_kv",
    )(mem, mem_norm.reshape(depth, 1, d), w_mem_kv)


def _mem_attention(qm, km_ref, vm_ref):
    acc = jnp.zeros((qm.shape[0], MEM_WIDTH), F32)
    for h in range(MEM_HEADS):
        s = lax.dot_general(qm, km_ref[h], NT_DIMS, preferred_element_type=F32)
        p = jnp.exp(s - jnp.max(s, axis=-1, keepdims=True))
        p = p / jnp.sum(p, axis=-1, keepdims=True)
        acc = acc + _dot(p.astype(BF16), vm_ref[h])
    return acc


def _proj_kernel(x_ref, gain_ref, w_ref, z_ref, *, col_chunk):
    h = _rms(x_ref[...], gain_ref[...]).astype(BF16)
    n = w_ref.shape[1]
    for c0 in range(0, n, col_chunk):
        c1 = min(c0 + col_chunk, n)
        z_ref[:, c0:c1] = _dot(h, w_ref[:, c0:c1]).astype(BF16)


def _proj_rope_kernel(x_ref, gain_ref, w_ref, pos_ref, invf_ref, rep_ref, pick_ref, spread_ref, z_ref, *,
                      rope_cols):
    h = _rms(x_ref[...], gain_ref[...]).astype(BF16)
    tm = x_ref.shape[0]
    half = ROPE_DIM // 2
    ang = pos_ref[...] * invf_ref[...]
    cs = jnp.concatenate([jnp.cos(ang), jnp.sin(ang)], axis=1)
    c1 = cs.astype(BF16)
    r1 = cs - c1.astype(F32)
    c2 = r1.astype(BF16)
    c3 = (r1 - c2.astype(F32)).astype(BF16)
    full = jnp.zeros((tm, 2 * LANES), F32)
    for piece in (c1, c2, c3):
        rows = _dot(rep_ref[...], piece) * pick_ref[...]
        full = full + _dot(rows.astype(BF16), spread_ref[...])
    lane = lax.broadcasted_iota(jnp.int32, (tm, LANES), 1) % HEAD_DIM
    cosf = jnp.where(lane < ROPE_DIM, full[:, :LANES], 1.0)
    sn = full[:, LANES:]
    sinf = jnp.where(lane < half, -sn, sn)
    first = lane < half
    n = w_ref.shape[1]
    wide = 2 * LANES
    for c0 in range(0, n, wide):
        zw = _dot(h, w_ref[:, c0:c0 + wide])
        for s0 in range(0, wide, LANES):
            zc = zw[:, s0:s0 + LANES]
            if c0 + s0 < rope_cols:
                partner = jnp.where(first, pltpu.roll(zc, LANES - half, 1), pltpu.roll(zc, half, 1))
                zc = zc * cosf + partner * sinf
            z_ref[:, c0 + s0:c0 + s0 + LANES] = zc.astype(BF16)


def _proj(x2, gain, w, tm, rope=None):
    t, d = x2.shape
    n = w.shape[1]
    in_specs = [
        pl.BlockSpec((tm, d), lambda i: (i, 0)),
        pl.BlockSpec((1, d), lambda i: (0, 0)),
        pl.BlockSpec((d, n), lambda i: (0, 0)),
    ]
    args = [x2, gain.reshape(1, d), w]
    if rope is None:
        kern = functools.partial(_proj_kernel, col_chunk=4 * MXU_WIDTH)
    else:
        pos, invf, rope_cols = rope
        kern = functools.partial(_proj_rope_kernel, rope_cols=rope_cols)
        half = ROPE_DIM // 2
        per_row = LANES // half
        ti = jnp.arange(tm, dtype=jnp.int32)[:, None]
        li = jnp.arange(LANES, dtype=jnp.int32)[None, :]
        rep = (jnp.arange(tm // per_row, dtype=jnp.int32)[None, :] == ti // per_row).astype(BF16)
        pick = jnp.tile((li // half == ti % per_row).astype(F32), (1, 2))
        spread = jnp.logical_and(li.T % half == li % half, li % HEAD_DIM < ROPE_DIM).astype(BF16)
        zero = jnp.zeros_like(spread)
        spread = jnp.block([[spread, zero], [zero, spread]])
        in_specs += [
            pl.BlockSpec((tm // per_row, LANES), lambda i: (i, 0)),
            pl.BlockSpec((1, LANES), lambda i: (0, 0)),
            pl.BlockSpec(rep.shape, lambda i: (0, 0)),
            pl.BlockSpec(pick.shape, lambda i: (0, 0)),
            pl.BlockSpec(spread.shape, lambda i: (0, 0)),
        ]
        args += [pos, invf, rep, pick, spread]
    return pl.pallas_call(
        kern,
        grid=(t // tm,),
        in_specs=in_specs,
        out_specs=pl.BlockSpec((tm, n), lambda i: (i, 0)),
        out_shape=jax.ShapeDtypeStruct((t, n), BF16),
        compiler_params=_params(("parallel",)),
        name="in_proj" if rope is None else "in_proj_rope",
    )(*args)


def _gla_mix_kernel(q_ref, k_ref, v_ref, r_ref, qm_ref, gl_ref, x_ref, gw_ref, gb_ref, on_ref, tri_ref, blk_ref,
                    km_ref, vm_ref, wo_ref, out_ref, qd_s, kd_s, ke_s, dec_s, o_s, sc_s, main_s, st_s):
    @pl.when(pl.program_id(1) == 0)
    def _():
        st_s[...] = jnp.zeros_like(st_s)

    tq = q_ref.shape[0]
    c_len = GLA_CHUNK
    n_chunks = tq // c_len
    logits = _dot(gl_ref[...], gw_ref[...]) + gb_ref[...]
    log_sig = jnp.minimum(logits, 0.0) - jnp.log(1.0 + jnp.exp(-jnp.abs(logits)))
    lane = lax.broadcasted_iota(jnp.int32, logits.shape, 1) % GLA_DK_PAD
    la = jnp.where(lane < GLA_DK, log_sig * (1.0 / GLA_TAU), 0.0)

    la_hi = la.astype(BF16)
    la_lo = (la - la_hi.astype(F32)).astype(BF16)
    tri = tri_ref[...]
    blk = blk_ref[...]
    b = _dot(tri, la_hi) + _dot(tri, la_lo)
    b_last = _dot(blk, la_hi) + _dot(blk, la_lo)
    qf = q_ref[...].astype(F32)
    kf = k_ref[...].astype(F32)
    qd_s[...] = (qf * (GLA_DK ** -0.5) * jnp.exp(b)).astype(BF16)
    kd_s[...] = (kf * jnp.exp(-b)).astype(BF16)
    ke_s[...] = (kf * jnp.exp(b_last - b)).astype(BF16)
    dec_s[...] = jnp.exp(b_last)

    kcols = [slice(h * GLA_DK_PAD, (h + 1) * GLA_DK_PAD) for h in range(GLA_HEADS)]
    vcols = [slice(h * GLA_DV_PAD, (h + 1) * GLA_DV_PAD) for h in range(GLA_HEADS)]

    sb = min(4 * c_len, tq)
    ri = lax.broadcasted_iota(jnp.int32, (sb, sb), 0)
    ci = lax.broadcasted_iota(jnp.int32, (sb, sb), 1)
    intra = jnp.logical_and(ri // c_len == ci // c_len, ci <= ri)
    for s0 in range(0, tq, sb):
        rows = slice(s0, s0 + sb)
        for h in range(GLA_HEADS):
            attn = lax.dot_general(qd_s[rows, kcols[h]], kd_s[rows, kcols[h]], NT_DIMS,
                                   preferred_element_type=F32)
            attn = jnp.where(intra, attn, 0.0).astype(BF16)
            o_s[rows, vcols[h]] = _dot(attn, v_ref[rows, vcols[h]])

    for c in range(n_chunks):
        rows = slice(c * c_len, (c + 1) * c_len)
        for h in range(GLA_HEADS):
            st = st_s[h]
            sc_s[c * GLA_HEADS + h] = st.astype(BF16)
            kv = lax.dot_general(v_ref[rows, vcols[h]], ke_s[rows, kcols[h]], TN_DIMS, preferred_element_type=F32)
            st_s[h] = st * dec_s[c * c_len:c * c_len + 1, kcols[h]] + kv
    for c in range(n_chunks):
        rows = slice(c * c_len, (c + 1) * c_len)
        for h in range(GLA_HEADS):
            o_s[rows, vcols[h]] += lax.dot_general(qd_s[rows, kcols[h]], sc_s[c * GLA_HEADS + h], NT_DIMS,
                                                   preferred_element_type=F32)

    for h in range(GLA_HEADS):
        o = o_s[:, vcols[h]]
        ms = jnp.sum(o * o, axis=-1, keepdims=True) * (1.0 / GLA_DV)
        o_n = o * lax.rsqrt(ms + EPS) * on_ref[...]
        main_s[:, vcols[h]] = (o_n * _silu(r_ref[:, vcols[h]].astype(F32))).astype(BF16)

    mem_out = _mem_attention(qm_ref[...], km_ref, vm_ref)
    n_main = GLA_HEADS * GLA_DV_PAD
    mixed = _dot(main_s[...], wo_ref[0:n_main, :]) + _dot(mem_out.astype(BF16), wo_ref[n_main:, :])
    out_ref[...] = x_ref[...] + mixed


def _gla_mix(z, x2, gate_w, gate_b, out_gain, km, vm, wo, b, s, tq):
    t, d = x2.shape
    nj = s // tq
    wq = GLA_HEADS * GLA_DK_PAD
    wv = GLA_HEADS * GLA_DV_PAD
    row = lambda bb, j: bb * nj + j
    qm_blk = (2 * wq + 2 * wv) // MEM_WIDTH
    gl_blk = (2 * wq + 2 * wv + MEM_WIDTH) // LANES
    n_mem = km.shape[2]
    in_specs = [
        pl.BlockSpec((tq, wq), lambda bb, j: (row(bb, j), 0)),
        pl.BlockSpec((tq, wq), lambda bb, j: (row(bb, j), 1)),
        pl.BlockSpec((tq, wv), lambda bb, j: (row(bb, j), 1)),
        pl.BlockSpec((tq, wv), lambda bb, j: (row(bb, j), 2)),
        pl.BlockSpec((tq, MEM_WIDTH), lambda bb, j: (row(bb, j), qm_blk)),
        pl.BlockSpec((tq, LANES), lambda bb, j: (row(bb, j), gl_blk)),
        pl.BlockSpec((tq, d), lambda bb, j: (row(bb, j), 0)),
        pl.BlockSpec(gate_w.shape, lambda bb, j: (0, 0)),
        pl.BlockSpec(gate_b.shape, lambda bb, j: (0, 0)),
        pl.BlockSpec(out_gain.shape, lambda bb, j: (0, 0)),
        pl.BlockSpec((tq, tq), lambda bb, j: (0, 0)),
        pl.BlockSpec((tq, tq), lambda bb, j: (0, 0)),
        pl.BlockSpec((None, MEM_HEADS, n_mem, MEM_WIDTH), lambda bb, j: (bb, 0, 0, 0)),
        pl.BlockSpec((None, MEM_HEADS, n_mem, MEM_WIDTH), lambda bb, j: (bb, 0, 0, 0)),
        pl.BlockSpec(wo.shape, lambda bb, j: (0, 0)),
    ]
    idx = jnp.arange(tq, dtype=jnp.int32)
    same = idx[:, None] // GLA_CHUNK == idx[None, :] // GLA_CHUNK
    tri = jnp.logical_and(same, idx[None, :] <= idx[:, None]).astype(BF16)
    blk = same.astype(BF16)
    n_chunks = tq // GLA_CHUNK
    return pl.pallas_call(
        _gla_mix_kernel,
        grid=(b, nj),
        in_specs=in_specs,
        out_specs=pl.BlockSpec((tq, d), lambda bb, j: (row(bb, j), 0)),
        out_shape=jax.ShapeDtypeStruct((t, d), F32),
        scratch_shapes=[
            pltpu.VMEM((tq, wq), BF16),
            pltpu.VMEM((tq, wq), BF16),
            pltpu.VMEM((tq, wq), BF16),
            pltpu.VMEM((tq, wq), F32),
            pltpu.VMEM((tq, wv), F32),
            pltpu.VMEM((n_chunks * GLA_HEADS, GLA_DV_PAD, GLA_DK_PAD), BF16),
            pltpu.VMEM((tq, wv), BF16),
            pltpu.VMEM((GLA_HEADS, GLA_DV_PAD, GLA_DK_PAD), F32),
        ],
        compiler_params=_params(("arbitrary", "arbitrary")),
        name="gla_mix",
    )(z, z, z, z, z, z, x2, gate_w, gate_b, out_gain, tri, blk, km, vm, wo)


def _swa_mix_kernel(sink_ref, q_ref, k_ref, v_ref, kp_ref, vp_ref, qm_ref, x_ref, km_ref, vm_ref,
                    wo_ref, out_ref, qt_s, kg_s, vt_s, s_s, p_s, maint_s):
    has_prev = pl.program_id(1) > 0
    tq = q_ref.shape[0]
    w = WINDOW
    group = SWA_HEADS // SWA_KV_HEADS
    qt_s[...] = jnp.transpose(q_ref[...].astype(F32)).astype(BF16)
    for g in range(SWA_KV_HEADS):
        gc = slice(g * HEAD_DIM, (g + 1) * HEAD_DIM)
        kg_s[g, 0:w, :] = kp_ref[:, gc]
        kg_s[g, w:, :] = k_ref[:, gc]
    vt_s[:, 0:w] = jnp.transpose(vp_ref[...].astype(F32)).astype(BF16)
    vt_s[:, w:] = jnp.transpose(v_ref[...].astype(F32)).astype(BF16)
    kj = lax.broadcasted_iota(jnp.int32, (2 * w, w), 0)
    qi = lax.broadcasted_iota(jnp.int32, (2 * w, w), 1)
    band = jnp.logical_and(kj > qi, kj <= qi + w)
    bias = jnp.where(band, 0.0, -jnp.inf)
    bias_first = jnp.where(jnp.logical_and(band, jnp.logical_or(kj >= w, has_prev)), 0.0, -jnp.inf)
    sink = sink_ref[...]
    for sub in range(tq // w):
        r0 = sub * w
        sub_bias = bias_first if sub == 0 else bias
        for h in range(SWA_HEADS):
            s_s[h] = _dot(kg_s[h // group, r0:r0 + 2 * w, :],
                          qt_s[h * HEAD_DIM:(h + 1) * HEAD_DIM, r0:r0 + w]) + sub_bias
        s = s_s[...]
        m = jnp.maximum(jnp.max(s, axis=1, keepdims=True), sink)
        p = jnp.exp(s - m)
        inv = 1.0 / (jnp.sum(p, axis=1, keepdims=True) + jnp.exp(sink - m))
        p_s[...] = p.astype(BF16)
        for h in range(SWA_HEADS):
            g = h // group
            o_t = _dot(vt_s[g * HEAD_DIM:(g + 1) * HEAD_DIM, r0:r0 + 2 * w], p_s[h])
            maint_s[h * HEAD_DIM:(h + 1) * HEAD_DIM, r0:r0 + w] = (o_t * inv[h]).astype(BF16)

    mem_out = _mem_attention(qm_ref[...], km_ref, vm_ref)
    n_main = SWA_HEADS * HEAD_DIM
    mixed = (lax.dot_general(maint_s[...], wo_ref[0:n_main, :], TN_DIMS, preferred_element_type=F32)
             + _dot(mem_out.astype(BF16), wo_ref[n_main:, :]))
    out_ref[...] = x_ref[...] + mixed


def _swa_mix(z, x2, sinks, km, vm, wo, b, s, tq):
    t, d = x2.shape
    nj = s // tq
    wq = SWA_HEADS * HEAD_DIM
    wkv = SWA_KV_HEADS * HEAD_DIM
    row = lambda bb, j: bb * nj + j
    per = tq // WINDOW
    prev = lambda bb, j: jnp.maximum(row(bb, j) * per - 1, 0)
    k_blk = wq // wkv
    n_mem = km.shape[2]
    sink_col = jnp.broadcast_to(sinks.astype(F32)[:, None, None], (SWA_HEADS, 1, WINDOW))
    in_specs = [
        pl.BlockSpec(sink_col.shape, lambda bb, j: (0, 0, 0)),
        pl.BlockSpec((tq, wq), lambda bb, j: (row(bb, j), 0)),
        pl.BlockSpec((tq, wkv), lambda bb, j: (row(bb, j), k_blk)),
        pl.BlockSpec((tq, wkv), lambda bb, j: (row(bb, j), k_blk + 1)),
        pl.BlockSpec((WINDOW, wkv), lambda bb, j: (prev(bb, j), k_blk)),
        pl.BlockSpec((WINDOW, wkv), lambda bb, j: (prev(bb, j), k_blk + 1)),
        pl.BlockSpec((tq, MEM_WIDTH), lambda bb, j: (row(bb, j), k_blk + 2)),
        pl.BlockSpec((tq, d), lambda bb, j: (row(bb, j), 0)),
        pl.BlockSpec((None, MEM_HEADS, n_mem, MEM_WIDTH), lambda bb, j: (bb, 0, 0, 0)),
        pl.BlockSpec((None, MEM_HEADS, n_mem, MEM_WIDTH), lambda bb, j: (bb, 0, 0, 0)),
        pl.BlockSpec(wo.shape, lambda bb, j: (0, 0)),
    ]
    return pl.pallas_call(
        _swa_mix_kernel,
        grid=(b, nj),
        in_specs=in_specs,
        out_specs=pl.BlockSpec((tq, d), lambda bb, j: (row(bb, j), 0)),
        out_shape=jax.ShapeDtypeStruct((t, d), F32),
        scratch_shapes=[
            pltpu.VMEM((wq, tq), BF16),
            pltpu.VMEM((SWA_KV_HEADS, tq + WINDOW, HEAD_DIM), BF16),
            pltpu.VMEM((wkv, tq + WINDOW), BF16),
            pltpu.VMEM((SWA_HEADS, 2 * WINDOW, WINDOW), F32),
            pltpu.VMEM((SWA_HEADS, 2 * WINDOW, WINDOW), BF16),
            pltpu.VMEM((wq, tq), BF16),
        ],
        compiler_params=_params(("parallel", "parallel")),
        name="swa_mix",
    )(sink_col, z, z, z, z, z, z, x2, km, vm, wo)


def _ffn_kernel(x_ref, gain_ref, wgu_ref, wd_ref, out_ref, *, ff_chunk):
    x = x_ref[...]
    h = _rms(x, gain_ref[...]).astype(BF16)
    d_ff = wd_ref.shape[0]
    acc = x
    for c0 in range(0, d_ff, ff_chunk):
        c1 = min(c0 + ff_chunk, d_ff)
        gate = _dot(h, wgu_ref[:, c0:c1])
        up = _dot(h, wgu_ref[:, d_ff + c0:d_ff + c1])
        acc = acc + _dot((_silu(gate) * up).astype(BF16), wd_ref[c0:c1, :])
    out_ref[...] = acc


def _ffn(x2, gain, w_gu, w_down, tm):
    t, d = x2.shape
    d_ff = w_down.shape[0]
    resident = pl.Buffered(1)
    return pl.pallas_call(
        functools.partial(_ffn_kernel, ff_chunk=6 * MXU_WIDTH),
        grid=(t // tm,),
        in_specs=[
            pl.BlockSpec((tm, d), lambda i: (i, 0)),
            pl.BlockSpec((1, d), lambda i: (0, 0)),
            pl.BlockSpec((d, 2 * d_ff), lambda i: (0, 0), pipeline_mode=resident),
            pl.BlockSpec((d_ff, d), lambda i: (0, 0), pipeline_mode=resident),
        ],
        out_specs=pl.BlockSpec((tm, d), lambda i: (i, 0)),
        out_shape=jax.ShapeDtypeStruct((t, d), F32),
        compiler_params=_params(("parallel",)),
        name="ffn_dense",
    )(x2, gain.reshape(1, d), w_gu, w_down)


def _router_kernel(x_ref, gain_ref, rw_ref, h_ref, ri_ref, rg_ref, cnt_ref, carry_s):
    @pl.when(pl.program_id(0) == 0)
    def _():
        carry_s[...] = jnp.zeros_like(carry_s)

    tb = x_ref.shape[0]
    h = _rms(x_ref[...], gain_ref[...])
    _store_token_tiles(h_ref, h)
    rw = rw_ref[...]
    h_hi, rw_hi = h.astype(BF16), rw.astype(BF16)
    h_lo = (h - h_hi.astype(F32)).astype(BF16)
    rw_lo = (rw - rw_hi.astype(F32)).astype(BF16)
    logits = _dot(h_hi, rw_hi) + (_dot(h_lo, rw_hi) + _dot(h_hi, rw_lo))
    lane = lax.broadcasted_iota(jnp.int32, logits.shape, 1)
    logits = jnp.where(lane < N_EXPERTS, logits, -jnp.inf)
    m1 = jnp.max(logits, axis=-1, keepdims=True)
    i1 = jnp.min(jnp.where(logits == m1, lane, LANES), axis=-1, keepdims=True)
    rest = jnp.where(lane == i1, -jnp.inf, logits)
    m2 = jnp.max(rest, axis=-1, keepdims=True)
    i2 = jnp.min(jnp.where(rest == m2, lane, LANES), axis=-1, keepdims=True)
    e2 = jnp.exp(m2 - m1)
    g1 = 1.0 / (1.0 + e2)
    g2 = e2 / (1.0 + e2)

    sel1 = lane == i1
    sel2 = lane == i2
    onehot = jnp.where(jnp.logical_or(sel1, sel2), 1.0, 0.0)
    ti = lax.broadcasted_iota(jnp.int32, (tb, tb), 0)
    tj = lax.broadcasted_iota(jnp.int32, (tb, tb), 1)
    before = jnp.where(tj < ti, 1.0, 0.0).astype(BF16)
    seen = carry_s[...] + _dot(before, onehot.astype(BF16))
    rank1 = jnp.sum(jnp.where(sel1, seen, 0.0), axis=-1, keepdims=True).astype(jnp.int32)
    rank2 = jnp.sum(jnp.where(sel2, seen, 0.0), axis=-1, keepdims=True).astype(jnp.int32)
    carry_s[...] += jnp.sum(onehot, axis=0, keepdims=True)
    cnt_ref[...] = carry_s[...].astype(jnp.int32)

    ri = jnp.where(lane == 0, i1, jnp.where(lane == 1, i2, jnp.where(lane == 2, rank1, rank2)))
    rg = jnp.where(lane == 0, g1, g2)
    ri_ref[...] = ri[:, :8]
    rg_ref[...] = rg[:, :8]


def _router(x2, gain, rw, tb):
    t, d = x2.shape
    return pl.pallas_call(
        _router_kernel,
        grid=(t // tb,),
        in_specs=[
            pl.BlockSpec((tb, d), lambda i: (i, 0)),
            pl.BlockSpec((1, d), lambda i: (0, 0)),
            pl.BlockSpec((d, LANES), lambda i: (0, 0)),
        ],
        out_specs=[
            pl.BlockSpec((tb * TILE_ROWS, LANES), lambda i: (i, 0)),
            pl.BlockSpec((tb, 8), lambda i: (i, 0)),
            pl.BlockSpec((tb, 8), lambda i: (i, 0)),
            pl.BlockSpec((1, LANES), lambda i: (0, 0)),
        ],
        out_shape=[
            jax.ShapeDtypeStruct((t * TILE_ROWS, LANES), F32),
            jax.ShapeDtypeStruct((t, 8), jnp.int32),
            jax.ShapeDtypeStruct((t, 8), F32),
            jax.ShapeDtypeStruct((1, LANES), jnp.int32),
        ],
        scratch_shapes=[pltpu.VMEM((1, LANES), F32)],
        compiler_params=_params(("arbitrary",)),
        name="router",
    )(x2, gain.reshape(1, d), rw)


def _tile_rows(i):
    return pl.ds(pl.multiple_of(i * TILE_ROWS, TILE_ROWS), TILE_ROWS)


def _dispatch_kernel(pad_lo_ref, pad_hi_ref, dest_ref, src_ref, dst_ref, sem):
    gb = dest_ref.shape[2] // 2

    def issue(t, carry):
        tile = src_ref.at[_tile_rows(t)]
        for k in range(2):
            pltpu.make_async_copy(tile, dst_ref.at[_tile_rows(dest_ref[0, 0, 2 * t + k])], sem).start(priority=k)
        return carry

    lax.fori_loop(0, gb, issue, 0, unroll=ISSUE_UNROLL)
    for _ in range(2):
        pltpu.make_async_copy(src_ref, dst_ref.at[pl.ds(0, gb * TILE_ROWS)], sem).wait()

    @pl.when(pl.program_id(0) == pl.num_programs(0) - 1)
    def _():
        for e in range(N_EXPERTS):
            lo, hi = pad_lo_ref[e], pad_hi_ref[e]

            def fill(r, carry):
                pltpu.make_async_copy(src_ref.at[_tile_rows(0)], dst_ref.at[_tile_rows(r)], sem).start()
                return carry

            def drain(r, carry):
                pltpu.make_async_copy(src_ref.at[_tile_rows(0)], dst_ref.at[_tile_rows(r)], sem).wait()
                return carry

            lax.fori_loop(lo, hi, fill, 0)
            lax.fori_loop(lo, hi, drain, 0)


def _dispatch(src, dest, pad_lo, pad_hi, n_rows, gb):
    t = src.shape[0] // TILE_ROWS
    return pl.pallas_call(
        _dispatch_kernel,
        grid_spec=pltpu.PrefetchScalarGridSpec(
            num_scalar_prefetch=2,
            grid=(t // gb,),
            in_specs=[
                pl.BlockSpec((1, 1, 2 * gb), lambda i, lo, hi: (i, 0, 0), memory_space=pltpu.SMEM),
                pl.BlockSpec((gb * TILE_ROWS, LANES), lambda i, lo, hi: (i, 0)),
            ],
            out_specs=pl.BlockSpec(memory_space=pl.ANY),
            scratch_shapes=[pltpu.SemaphoreType.DMA(())],
        ),
        out_shape=jax.ShapeDtypeStruct((n_rows * TILE_ROWS, LANES), src.dtype),
        compiler_params=_params(("arbitrary",)),
        name="moe_dispatch",
    )(pad_lo, pad_hi, dest.reshape(t // gb, 1, 2 * gb), src)


def _moe_kernel(te_ref, nu_ref, xs_ref, wgu_ref, wd_ref, out_ref, *, ff_chunk):
    used = pl.program_id(0) < nu_ref[0]

    @pl.when(used)
    def _():
        tm = xs_ref.shape[0] // TILE_ROWS
        h = jnp.concatenate([part.astype(BF16) for part in _load_token_tiles(xs_ref, tm)], axis=1)
        d_ff = wd_ref.shape[1]
        acc = None
        for c0 in range(0, d_ff, ff_chunk):
            c1 = min(c0 + ff_chunk, d_ff)
            gate = _dot(h, wgu_ref[0, :, c0:c1])
            up = _dot(h, wgu_ref[0, :, d_ff + c0:d_ff + c1])
            part = _dot((_silu(gate) * up).astype(BF16), wd_ref[0, c0:c1, :])
            acc = part if acc is None else acc + part
        _store_token_tiles(out_ref, acc)

    @pl.when(jnp.logical_not(used))
    def _():
        out_ref[...] = jnp.zeros_like(out_ref)


def _moe(xs, tile_expert, n_used, w_gu, w_down, tm):
    n_rows = xs.shape[0] // TILE_ROWS
    d = w_down.shape[2]
    d_ff = w_down.shape[1]
    nt = n_rows // tm
    resident = pl.Buffered(1)
    return pl.pallas_call(
        functools.partial(_moe_kernel, ff_chunk=7 * MXU_WIDTH),
        grid_spec=pltpu.PrefetchScalarGridSpec(
            num_scalar_prefetch=2,
            grid=(nt,),
            in_specs=[
                pl.BlockSpec((tm * TILE_ROWS, LANES), lambda i, te, nu: (jnp.minimum(i, nu[0] - 1), 0)),
                pl.BlockSpec((1, d, 2 * d_ff), lambda i, te, nu: (te[i], 0, 0), pipeline_mode=resident),
                pl.BlockSpec((1, d_ff, d), lambda i, te, nu: (te[i], 0, 0), pipeline_mode=resident),
            ],
            out_specs=pl.BlockSpec((tm * TILE_ROWS, LANES), lambda i, te, nu: (i, 0)),
        ),
        out_shape=jax.ShapeDtypeStruct((n_rows * TILE_ROWS, LANES), F32),
        compiler_params=_params(("arbitrary",)),
        name="moe_experts",
    )(tile_expert, n_used, xs, w_gu, w_down)


def _combine_kernel(dest_ref, x_ref, rg_ref, gain_ref, ys_ref, out_ref, buf, sem):
    tb = x_ref.shape[0]

    def issue(t, carry):
        for k in range(2):
            pltpu.make_async_copy(ys_ref.at[_tile_rows(dest_ref[0, 0, 2 * t + k])],
                                  buf.at[k, _tile_rows(t)], sem).start(priority=k)
        return carry

    lax.fori_loop(0, tb, issue, 0, unroll=ISSUE_UNROLL)
    for k in range(2):
        pltpu.make_async_copy(ys_ref.at[pl.ds(0, tb * TILE_ROWS)], buf.at[k], sem).wait()

    g = rg_ref[...]
    g1, g2 = g[:, 0:1], g[:, 1:2]
    ya = _load_token_tiles(buf.at[0], tb)
    yb = _load_token_tiles(buf.at[1], tb)
    y = jnp.concatenate([g1 * a + g2 * b for a, b in zip(ya, yb)], axis=1)
    out_ref[...] = _rms(x_ref[...] + y, gain_ref[...])


def _combine(x2, ys, dest, rg, gain, tb):
    t, d = x2.shape
    return pl.pallas_call(
        _combine_kernel,
        grid=(t // tb,),
        in_specs=[
            pl.BlockSpec((1, 1, 2 * tb), lambda i: (i, 0, 0), memory_space=pltpu.SMEM),
            pl.BlockSpec((tb, d), lambda i: (i, 0)),
            pl.BlockSpec((tb, 8), lambda i: (i, 0)),
            pl.BlockSpec((1, d), lambda i: (0, 0)),
            pl.BlockSpec(memory_space=pl.ANY),
        ],
        out_specs=pl.BlockSpec((tb, d), lambda i: (i, 0)),
        out_shape=jax.ShapeDtypeStruct((t, d), F32),
        scratch_shapes=[pltpu.VMEM((2, tb * TILE_ROWS, LANES), F32), pltpu.SemaphoreType.DMA(())],
        compiler_params=_params(("arbitrary",)),
        name="moe_combine",
    )(dest.reshape(t // tb, 1, 2 * tb), x2, rg, gain.reshape(1, d), ys)


def _pad_heads(w, heads, width, padded):
    lead = w.shape[:-1]
    w = w.reshape(lead + (heads, width))
    w = jnp.pad(w, [(0, 0)] * len(lead) + [(0, 0), (0, padded - width)])
    return w.reshape(lead + (heads * padded,))


def _layer_a_weights(w_in, gate_w, gate_b, out_norm, w_out):
    dqk = GLA_HEADS * GLA_DK
    dv = GLA_HEADS * GLA_DV
    o = 0
    wq = w_in[:, o:o + dqk]; o += dqk
    wk = w_in[:, o:o + dqk]; o += dqk
    wv = w_in[:, o:o + dv]; o += dv
    wl = w_in[:, o:o + GLA_LOWRANK]; o += GLA_LOWRANK
    wr = w_in[:, o:o + dv]; o += dv
    wm = w_in[:, o:o + MEM_WIDTH]
    w = jnp.concatenate([
        _pad_heads(wq, GLA_HEADS, GLA_DK, GLA_DK_PAD),
        _pad_heads(wk, GLA_HEADS, GLA_DK, GLA_DK_PAD),
        _pad_heads(wv, GLA_HEADS, GLA_DV, GLA_DV_PAD),
        _pad_heads(wr, GLA_HEADS, GLA_DV, GLA_DV_PAD),
        wm,
        jnp.pad(wl, ((0, 0), (0, LANES - GLA_LOWRANK))),
    ], axis=1).astype(BF16)
    gw = _pad_heads(gate_w, GLA_HEADS, GLA_DK, GLA_DK_PAD)
    gw = jnp.pad(gw, ((0, LANES - GLA_LOWRANK), (0, 0))).astype(BF16)
    gb = _pad_heads(gate_b, GLA_HEADS, GLA_DK, GLA_DK_PAD).reshape(1, -1)
    og = jnp.pad(out_norm, (0, GLA_DV_PAD - GLA_DV)).reshape(1, -1)
    wo_main = w_out[:dv].reshape(GLA_HEADS, GLA_DV, -1)
    wo_main = jnp.pad(wo_main, ((0, 0), (0, GLA_DV_PAD - GLA_DV), (0, 0))).reshape(GLA_HEADS * GLA_DV_PAD, -1)
    wo = jnp.concatenate([wo_main, w_out[dv:]], axis=0).astype(BF16)
    return w, gw, gb, og, wo


def kernel(x, mem, positions, mix_norm, w_in_a, gla_gate_w, gla_gate_b, gla_out_norm, w_in_b, swa_sinks,
           mem_norm, w_mem_kv, w_out, ffn_norm, ffn_w_gate_up, ffn_w_down, router_w, exp_w_gate_up,
           exp_w_down, final_norm):
    b, s, d = x.shape
    t = b * s
    x2 = x.reshape(t, d)
    assert d == TILE_ROWS * LANES
    tq = min(512, s)
    tm_moe = 512
    gb = min(1024, t)

    km, vm = _memkv(mem, mem_norm, w_mem_kv)

    wa, gw, gbias, og, wo_a = _layer_a_weights(w_in_a[0], gla_gate_w[0], gla_gate_b[0], gla_out_norm[0], w_out[0])
    z = _proj(x2, mix_norm[0], wa, tq)
    x2 = _gla_mix(z, x2, gw, gbias, og, km[0], vm[0], wo_a, b, s, tq)
    x2 = _ffn(x2, ffn_norm[0], ffn_w_gate_up[0].astype(BF16), ffn_w_down[0].astype(BF16), tq)

    n_q = SWA_HEADS * HEAD_DIM
    n_kv = SWA_KV_HEADS * HEAD_DIM
    wb = jnp.concatenate([w_in_b[0][:, :n_q] * (HEAD_DIM ** -0.5), w_in_b[0][:, n_q:]], axis=1).astype(BF16)
    half = ROPE_DIM // 2
    inv_freq = jnp.power(jnp.float32(ROPE_THETA), -jnp.arange(half, dtype=F32) / half)
    invf = jnp.tile(inv_freq, LANES // half).reshape(1, LANES)
    pos = jnp.repeat(positions.astype(F32).reshape(t), half).reshape(t * half // LANES, LANES)
    z = _proj(x2, mix_norm[1], wb, tq, rope=(pos, invf, n_q + n_kv))
    x2 = _swa_mix(z, x2, swa_sinks[0], km[1], vm[1], w_out[1].astype(BF16), b, s, tq)

    rw = jnp.pad(router_w[0], ((0, 0), (0, LANES - N_EXPERTS)))
    h, ri, rg, counts = _router(x2, ffn_norm[1], rw, tq)

    cnt = counts[0, :N_EXPERTS]
    padded = ((cnt + tm_moe - 1) // tm_moe) * tm_moe
    ends = jnp.cumsum(padded)
    offs = ends - padded
    n_tiles = 2 * t // tm_moe + N_EXPERTS
    n_used = (ends[-1] // tm_moe).astype(jnp.int32)
    tile_start = jnp.arange(n_tiles, dtype=jnp.int32) * tm_moe
    tile_expert = jnp.sum(tile_start[:, None] >= ends[None, :], axis=1).astype(jnp.int32)
    tile_expert = jnp.minimum(tile_expert, N_EXPERTS - 1)
    last_expert = tile_expert[jnp.maximum(n_used - 1, 0)]
    tile_expert = jnp.where(jnp.arange(n_tiles) < n_used, tile_expert, last_expert)
    experts = ri[:, 0:2]
    off_of = jnp.sum(jnp.where(experts[:, :, None] == jnp.arange(N_EXPERTS)[None, None, :],
                               offs[None, None, :], 0), axis=-1)
    dest = (off_of + ri[:, 2:4]).astype(jnp.int32)

    pad_hi = ends.at[N_EXPERTS - 1].set(n_tiles * tm_moe)
    xs = _dispatch(h, dest, (offs + cnt).astype(jnp.int32), pad_hi.astype(jnp.int32), n_tiles * tm_moe, gb)
    ys = _moe(xs, tile_expert, n_used.reshape(1), exp_w_gate_up[0].astype(BF16), exp_w_down[0].astype(BF16),
              tm_moe)
    out = _combine(x2, ys, dest, rg, final_norm, gb)
    return out.reshape(b, s, d)
```

```python
import functools

import jax
import jax.numpy as jnp
from jax import lax
from jax.experimental import pallas as pl
from jax.experimental.pallas import tpu as pltpu

F32 = jnp.float32
BF16 = jnp.bfloat16

HEAD_DIM = 64
MEM_HEADS = 4
MEM_WIDTH = MEM_HEADS * HEAD_DIM
GLA_HEADS = 4
GLA_DK = 96
GLA_DV = 192
GLA_LOWRANK = 16
GLA_TAU = 16.0
GLA_CHUNK = 64
SWA_HEADS = 12
SWA_KV_HEADS = 4
WINDOW = 128
ROPE_DIM = 16
ROPE_THETA = 500000.0
N_EXPERTS = 8
EPS = 1e-6

LANES = 128
TILE_ROWS = 8
GLA_DK_PAD = 128
GLA_DV_PAD = 256
VMEM_LIMIT = 56 * 1024 * 1024
ISSUE_UNROLL = 4
MXU_WIDTH = 256

NT_DIMS = (((1,), (1,)), ((), ()))
TN_DIMS = (((0,), (0,)), ((), ()))


def _params(sem):
    return pltpu.CompilerParams(dimension_semantics=sem, vmem_limit_bytes=VMEM_LIMIT)


def _rms(xf, gain):
    return xf * lax.rsqrt(jnp.mean(xf * xf, axis=-1, keepdims=True) + EPS) * gain


def _dot(a, b):
    return jnp.dot(a, b, preferred_element_type=F32)


def _silu(x):
    return x * jax.nn.sigmoid(x)


def _store_token_tiles(ref, val):
    n = val.shape[0]
    for c in range(TILE_ROWS):
        ref[pl.ds(c, n, stride=TILE_ROWS), :] = val[:, c * LANES:(c + 1) * LANES].astype(ref.dtype)


def _load_token_tiles(ref, n):
    return [ref[pl.ds(c, n, stride=TILE_ROWS), :] for c in range(TILE_ROWS)]


def _memkv_kernel(mem_ref, gain_ref, w_ref, km_ref, vm_ref, vmt_ref):
    mem_n = _rms(mem_ref[0], gain_ref[0]).astype(BF16)
    kv = _dot(mem_n, w_ref[0].astype(BF16))
    k = kv[:, :MEM_WIDTH] * (HEAD_DIM ** -0.5)
    v = kv[:, MEM_WIDTH:]
    head = lax.broadcasted_iota(jnp.int32, k.shape, 1) // HEAD_DIM
    for h in range(MEM_HEADS):
        km_ref[0, 0, h] = jnp.where(head == h, k, 0.0).astype(BF16)
        v_h = jnp.where(head == h, v, 0.0)
        vm_ref[0, 0, h] = v_h.astype(BF16)
        vmt_ref[0, 0, h] = jnp.transpose(v_h).astype(BF16)


def _memkv(mem, mem_norm, w_mem_kv):
    depth = mem_norm.shape[0]
    b, n_mem, d = mem.shape
    out = jax.ShapeDtypeStruct((depth, b, MEM_HEADS, n_mem, MEM_WIDTH), BF16)
    out_t = jax.ShapeDtypeStruct((depth, b, MEM_HEADS, MEM_WIDTH, n_mem), BF16)
    blk = pl.BlockSpec((1, 1, MEM_HEADS, n_mem, MEM_WIDTH), lambda i, bb: (i, bb, 0, 0, 0))
    blk_t = pl.BlockSpec((1, 1, MEM_HEADS, MEM_WIDTH, n_mem), lambda i, bb: (i, bb, 0, 0, 0))
    return pl.pallas_call(
        _memkv_kernel,
        grid=(depth, b),
        in_specs=[
            pl.BlockSpec((1, n_mem, d), lambda i, bb: (bb, 0, 0)),
            pl.BlockSpec((1, 1, d), lambda i, bb: (i, 0, 0)),
            pl.BlockSpec((1, d, 2 * MEM_WIDTH), lambda i, bb: (i, 0, 0)),
        ],
        out_specs=[blk, blk, blk_t],
        out_shape=[out, out, out_t],
        compiler_params=_params(("arbitrary", "arbitrary")),
        name="mem_kv",
    )(mem, mem_norm.reshape(depth, 1, d), w_mem_kv)


def _mem_attention(qm, km_ref, vm_ref):
    acc = jnp.zeros((qm.shape[0], MEM_WIDTH), F32)
    for h in range(MEM_HEADS):
        s = lax.dot_general(qm, km_ref[h], NT_DIMS, preferred_element_type=F32)
        p = jnp.exp(s - jnp.max(s, axis=-1, keepdims=True))
        p = p / jnp.sum(p, axis=-1, keepdims=True)
        acc = acc + _dot(p.astype(BF16), vm_ref[h])
    return acc


def _mem_attention_t(qm, km_ref, vmt_ref):
    qm_t = jnp.transpose(qm.astype(F32)).astype(BF16)
    acc = jnp.zeros(qm_t.shape, F32)
    for h in range(MEM_HEADS):
        s = _dot(km_ref[h], qm_t)
        p = jnp.exp(s - jnp.max(s, axis=0, keepdims=True))
        inv = 1.0 / jnp.sum(p, axis=0, keepdims=True)
        acc = acc + _dot(vmt_ref[h], p.astype(BF16)) * inv
    return acc


def _proj_kernel(x_ref, gain_ref, w_ref, z_ref, *, col_chunk):
    h = _rms(x_ref[...], gain_ref[...]).astype(BF16)
    n = w_ref.shape[1]
    for c0 in range(0, n, col_chunk):
        c1 = min(c0 + col_chunk, n)
        z_ref[:, c0:c1] = _dot(h, w_ref[:, c0:c1]).astype(BF16)


def _proj_rope_kernel(x_ref, gain_ref, w_ref, pos_ref, invf_ref, rep_ref, pick_ref, spread_ref, z_ref, *,
                      rope_cols):
    h = _rms(x_ref[...], gain_ref[...]).astype(BF16)
    tm = x_ref.shape[0]
    half = ROPE_DIM // 2
    ang = pos_ref[...] * invf_ref[...]
    cs = jnp.concatenate([jnp.cos(ang), jnp.sin(ang)], axis=1)
    c1 = cs.astype(BF16)
    r1 = cs - c1.astype(F32)
    c2 = r1.astype(BF16)
    c3 = (r1 - c2.astype(F32)).astype(BF16)
    full = jnp.zeros((tm, 2 * LANES), F32)
    for piece in (c1, c2, c3):
        rows = _dot(rep_ref[...], piece) * pick_ref[...]
        full = full + _dot(rows.astype(BF16), spread_ref[...])
    lane = lax.broadcasted_iota(jnp.int32, (tm, LANES), 1) % HEAD_DIM
    cosf = jnp.where(lane < ROPE_DIM, full[:, :LANES], 1.0)
    sn = full[:, LANES:]
    sinf = jnp.where(lane < half, -sn, sn)
    first = lane < half
    n = w_ref.shape[1]
    wide = 2 * LANES
    for c0 in range(0, n, wide):
        zw = _dot(h, w_ref[:, c0:c0 + wide])
        for s0 in range(0, wide, LANES):
            zc = zw[:, s0:s0 + LANES]
            if c0 + s0 < rope_cols:
                partner = jnp.where(first, pltpu.roll(zc, LANES - half, 1), pltpu.roll(zc, half, 1))
                zc = zc * cosf + partner * sinf
            z_ref[:, c0 + s0:c0 + s0 + LANES] = zc.astype(BF16)


def _proj(x2, gain, w, tm, rope=None):
    t, d = x2.shape
    n = w.shape[1]
    in_specs = [
        pl.BlockSpec((tm, d), lambda i: (i, 0)),
        pl.BlockSpec((1, d), lambda i: (0, 0)),
        pl.BlockSpec((d, n), lambda i: (0, 0)),
    ]
    args = [x2, gain.reshape(1, d), w]
    if rope is None:
        kern = functools.partial(_proj_kernel, col_chunk=4 * MXU_WIDTH)
    else:
        pos, invf, rope_cols = rope
        kern = functools.partial(_proj_rope_kernel, rope_cols=rope_cols)
        half = ROPE_DIM // 2
        per_row = LANES // half
        ti = jnp.arange(tm, dtype=jnp.int32)[:, None]
        li = jnp.arange(LANES, dtype=jnp.int32)[None, :]
        rep = (jnp.arange(tm // per_row, dtype=jnp.int32)[None, :] == ti // per_row).astype(BF16)
        pick = jnp.tile((li // half == ti % per_row).astype(F32), (1, 2))
        spread = jnp.logical_and(li.T % half == li % half, li % HEAD_DIM < ROPE_DIM).astype(BF16)
        zero = jnp.zeros_like(spread)
        spread = jnp.block([[spread, zero], [zero, spread]])
        in_specs += [
            pl.BlockSpec((tm // per_row, LANES), lambda i: (i, 0)),
            pl.BlockSpec((1, LANES), lambda i: (0, 0)),
            pl.BlockSpec(rep.shape, lambda i: (0, 0)),
            pl.BlockSpec(pick.shape, lambda i: (0, 0)),
            pl.BlockSpec(spread.shape, lambda i: (0, 0)),
        ]
        args += [pos, invf, rep, pick, spread]
    return pl.pallas_call(
        kern,
        grid=(t // tm,),
        in_specs=in_specs,
        out_specs=pl.BlockSpec((tm, n), lambda i: (i, 0)),
        out_shape=jax.ShapeDtypeStruct((t, n), BF16),
        compiler_params=_params(("parallel",)),
        name="in_proj" if rope is None else "in_proj_rope",
    )(*args)


def _gla_mix_kernel(q_ref, k_ref, v_ref, r_ref, qm_ref, gl_ref, x_ref, gw_ref, gb_ref, on_ref, tri_ref,
                    km_ref, vm_ref, wo_ref, out_ref, qd_s, kd_s, ke_s, dec_s, o_s, sc_s, main_s, st_s):
    @pl.when(pl.program_id(1) == 0)
    def _():
        st_s[...] = jnp.zeros_like(st_s)

    tq = q_ref.shape[0]
    c_len = GLA_CHUNK
    n_chunks = tq // c_len
    logits = _dot(gl_ref[...], gw_ref[...]) + gb_ref[...]
    log_sig = jnp.minimum(logits, 0.0) - jnp.log(1.0 + jnp.exp(-jnp.abs(logits)))
    lane = lax.broadcasted_iota(jnp.int32, logits.shape, 1) % GLA_DK_PAD
    la = jnp.where(lane < GLA_DK, log_sig * (1.0 / GLA_TAU), 0.0)

    la_hi = la.astype(BF16)
    la_lo = (la - la_hi.astype(F32)).astype(BF16)
    tri = tri_ref[...]
    sbr = tri.shape[0]
    b = jnp.concatenate([_dot(tri, la_hi[s0:s0 + sbr]) + _dot(tri, la_lo[s0:s0 + sbr])
                         for s0 in range(0, tq, sbr)], axis=0)
    b3 = b.reshape(n_chunks, c_len, b.shape[1])
    b_last = jnp.broadcast_to(b3[:, c_len - 1:c_len, :], b3.shape).reshape(b.shape)
    qf = q_ref[...].astype(F32)
    kf = k_ref[...].astype(F32)
    qd_s[...] = (qf * (GLA_DK ** -0.5) * jnp.exp(b)).astype(BF16)
    kd_s[...] = (kf * jnp.exp(-b)).astype(BF16)
    ke_s[...] = (kf * jnp.exp(b_last - b)).astype(BF16)
    dec_s[...] = jnp.exp(b_last)

    kcols = [slice(h * GLA_DK_PAD, (h + 1) * GLA_DK_PAD) for h in range(GLA_HEADS)]
    vcols = [slice(h * GLA_DV_PAD, (h + 1) * GLA_DV_PAD) for h in range(GLA_HEADS)]

    sb = min(4 * c_len, tq)
    ri = lax.broadcasted_iota(jnp.int32, (sb, sb), 0)
    ci = lax.broadcasted_iota(jnp.int32, (sb, sb), 1)
    intra = jnp.logical_and(ri // c_len == ci // c_len, ci <= ri)
    for s0 in range(0, tq, sb):
        rows = slice(s0, s0 + sb)
        for h in range(GLA_HEADS):
            attn = lax.dot_general(qd_s[rows, kcols[h]], kd_s[rows, kcols[h]], NT_DIMS,
                                   preferred_element_type=F32)
            attn = jnp.where(intra, attn, 0.0).astype(BF16)
            o_s[rows, vcols[h]] = _dot(attn, v_ref[rows, vcols[h]])

    for c in range(n_chunks):
        rows = slice(c * c_len, (c + 1) * c_len)
        for h in range(GLA_HEADS):
            st = st_s[h]
            sc_s[c * GLA_HEADS + h] = st.astype(BF16)
            kv = lax.dot_general(v_ref[rows, vcols[h]], ke_s[rows, kcols[h]], TN_DIMS, preferred_element_type=F32)
            st_s[h] = st * dec_s[c * c_len:c * c_len + 1, kcols[h]] + kv
    for c in range(n_chunks):
        rows = slice(c * c_len, (c + 1) * c_len)
        for h in range(GLA_HEADS):
            o_s[rows, vcols[h]] += lax.dot_general(qd_s[rows, kcols[h]], sc_s[c * GLA_HEADS + h], NT_DIMS,
                                                   preferred_element_type=F32)

    for h in range(GLA_HEADS):
        o = o_s[:, vcols[h]]
        ms = jnp.sum(o * o, axis=-1, keepdims=True) * (1.0 / GLA_DV)
        o_n = o * lax.rsqrt(ms + EPS) * on_ref[...]
        main_s[:, vcols[h]] = (o_n * _silu(r_ref[:, vcols[h]].astype(F32))).astype(BF16)

    mem_out = _mem_attention(qm_ref[...], km_ref, vm_ref)
    n_main = GLA_HEADS * GLA_DV_PAD
    mixed = _dot(main_s[...], wo_ref[0:n_main, :]) + _dot(mem_out.astype(BF16), wo_ref[n_main:, :])
    out_ref[...] = x_ref[...] + mixed


def _gla_mix(z, x2, gate_w, gate_b, out_gain, km, vm, wo, b, s, tq):
    t, d = x2.shape
    nj = s // tq
    wq = GLA_HEADS * GLA_DK_PAD
    wv = GLA_HEADS * GLA_DV_PAD
    row = lambda bb, j: bb * nj + j
    qm_blk = (2 * wq + 2 * wv) // MEM_WIDTH
    gl_blk = (2 * wq + 2 * wv + MEM_WIDTH) // LANES
    n_mem = km.shape[2]
    sbr = min(MXU_WIDTH, tq)
    in_specs = [
        pl.BlockSpec((tq, wq), lambda bb, j: (row(bb, j), 0)),
        pl.BlockSpec((tq, wq), lambda bb, j: (row(bb, j), 1)),
        pl.BlockSpec((tq, wv), lambda bb, j: (row(bb, j), 1)),
        pl.BlockSpec((tq, wv), lambda bb, j: (row(bb, j), 2)),
        pl.BlockSpec((tq, MEM_WIDTH), lambda bb, j: (row(bb, j), qm_blk)),
        pl.BlockSpec((tq, LANES), lambda bb, j: (row(bb, j), gl_blk)),
        pl.BlockSpec((tq, d), lambda bb, j: (row(bb, j), 0)),
        pl.BlockSpec(gate_w.shape, lambda bb, j: (0, 0)),
        pl.BlockSpec(gate_b.shape, lambda bb, j: (0, 0)),
        pl.BlockSpec(out_gain.shape, lambda bb, j: (0, 0)),
        pl.BlockSpec((sbr, sbr), lambda bb, j: (0, 0)),
        pl.BlockSpec((None, MEM_HEADS, n_mem, MEM_WIDTH), lambda bb, j: (bb, 0, 0, 0)),
        pl.BlockSpec((None, MEM_HEADS, n_mem, MEM_WIDTH), lambda bb, j: (bb, 0, 0, 0)),
        pl.BlockSpec(wo.shape, lambda bb, j: (0, 0)),
    ]
    idx = jnp.arange(sbr, dtype=jnp.int32)
    same = idx[:, None] // GLA_CHUNK == idx[None, :] // GLA_CHUNK
    tri = jnp.logical_and(same, idx[None, :] <= idx[:, None]).astype(BF16)
    n_chunks = tq // GLA_CHUNK
    return pl.pallas_call(
        _gla_mix_kernel,
        grid=(b, nj),
        in_specs=in_specs,
        out_specs=pl.BlockSpec((tq, d), lambda bb, j: (row(bb, j), 0)),
        out_shape=jax.ShapeDtypeStruct((t, d), F32),
        scratch_shapes=[
            pltpu.VMEM((tq, wq), BF16),
            pltpu.VMEM((tq, wq), BF16),
            pltpu.VMEM((tq, wq), BF16),
            pltpu.VMEM((tq, wq), F32),
            pltpu.VMEM((tq, wv), F32),
            pltpu.VMEM((n_chunks * GLA_HEADS, GLA_DV_PAD, GLA_DK_PAD), BF16),
            pltpu.VMEM((tq, wv), BF16),
            pltpu.VMEM((GLA_HEADS, GLA_DV_PAD, GLA_DK_PAD), F32),
        ],
        compiler_params=_params(("arbitrary", "arbitrary")),
        name="gla_mix",
    )(z, z, z, z, z, z, x2, gate_w, gate_b, out_gain, tri, km, vm, wo)


def _swa_mix_kernel(sink_ref, q_ref, k_ref, v_ref, kp_ref, vp_ref, qm_ref, x_ref, km_ref, vmt_ref,
                    wo_ref, out_ref, qt_s, kg_s, vt_s, s_s, p_s, maint_s):
    has_prev = pl.program_id(1) > 0
    tq = q_ref.shape[0]
    w = WINDOW
    group = SWA_HEADS // SWA_KV_HEADS
    qt_s[...] = jnp.transpose(q_ref[...].astype(F32)).astype(BF16)
    for g in range(SWA_KV_HEADS):
        gc = slice(g * HEAD_DIM, (g + 1) * HEAD_DIM)
        kg_s[g, 0:w, :] = kp_ref[:, gc]
        kg_s[g, w:, :] = k_ref[:, gc]
    vt_s[:, 0:w] = jnp.transpose(vp_ref[...].astype(F32)).astype(BF16)
    vt_s[:, w:] = jnp.transpose(v_ref[...].astype(F32)).astype(BF16)
    kj = lax.broadcasted_iota(jnp.int32, (2 * w, w), 0)
    qi = lax.broadcasted_iota(jnp.int32, (2 * w, w), 1)
    band = jnp.logical_and(kj > qi, kj <= qi + w)
    bias = jnp.where(band, 0.0, -jnp.inf)
    bias_first = jnp.where(jnp.logical_and(band, jnp.logical_or(kj >= w, has_prev)), 0.0, -jnp.inf)
    sink = sink_ref[...]
    for sub in range(tq // w):
        r0 = sub * w
        sub_bias = bias_first if sub == 0 else bias
        for h in range(SWA_HEADS):
            s_s[h] = _dot(kg_s[h // group, r0:r0 + 2 * w, :],
                          qt_s[h * HEAD_DIM:(h + 1) * HEAD_DIM, r0:r0 + w]) + sub_bias
        s = s_s[...]
        m = jnp.maximum(jnp.max(s, axis=1, keepdims=True), sink)
        p = jnp.exp(s - m)
        inv = 1.0 / (jnp.sum(p, axis=1, keepdims=True) + jnp.exp(sink - m))
        p_s[...] = p.astype(BF16)
        for h in range(SWA_HEADS):
            g = h // group
            o_t = _dot(vt_s[g * HEAD_DIM:(g + 1) * HEAD_DIM, r0:r0 + 2 * w], p_s[h])
            maint_s[h * HEAD_DIM:(h + 1) * HEAD_DIM, r0:r0 + w] = (o_t * inv[h]).astype(BF16)

    mem_out_t = _mem_attention_t(qm_ref[...], km_ref, vmt_ref)
    n_main = SWA_HEADS * HEAD_DIM
    mixed = (lax.dot_general(maint_s[...], wo_ref[0:n_main, :], TN_DIMS, preferred_element_type=F32)
             + lax.dot_general(mem_out_t.astype(BF16), wo_ref[n_main:, :], TN_DIMS, preferred_element_type=F32))
    out_ref[...] = x_ref[...] + mixed


def _swa_mix(z, x2, sinks, km, vmt, wo, b, s, tq):
    t, d = x2.shape
    nj = s // tq
    wq = SWA_HEADS * HEAD_DIM
    wkv = SWA_KV_HEADS * HEAD_DIM
    row = lambda bb, j: bb * nj + j
    per = tq // WINDOW
    prev = lambda bb, j: jnp.maximum(row(bb, j) * per - 1, 0)
    k_blk = wq // wkv
    n_mem = km.shape[2]
    sink_col = jnp.broadcast_to(sinks.astype(F32)[:, None, None], (SWA_HEADS, 1, WINDOW))
    in_specs = [
        pl.BlockSpec(sink_col.shape, lambda bb, j: (0, 0, 0)),
        pl.BlockSpec((tq, wq), lambda bb, j: (row(bb, j), 0)),
        pl.BlockSpec((tq, wkv), lambda bb, j: (row(bb, j), k_blk)),
        pl.BlockSpec((tq, wkv), lambda bb, j: (row(bb, j), k_blk + 1)),
        pl.BlockSpec((WINDOW, wkv), lambda bb, j: (prev(bb, j), k_blk)),
        pl.BlockSpec((WINDOW, wkv), lambda bb, j: (prev(bb, j), k_blk + 1)),
        pl.BlockSpec((tq, MEM_WIDTH), lambda bb, j: (row(bb, j), k_blk + 2)),
        pl.BlockSpec((tq, d), lambda bb, j: (row(bb, j), 0)),
        pl.BlockSpec((None, MEM_HEADS, n_mem, MEM_WIDTH), lambda bb, j: (bb, 0, 0, 0)),
        pl.BlockSpec((None, MEM_HEADS, MEM_WIDTH, n_mem), lambda bb, j: (bb, 0, 0, 0)),
        pl.BlockSpec(wo.shape, lambda bb, j: (0, 0)),
    ]
    return pl.pallas_call(
        _swa_mix_kernel,
        grid=(b, nj),
        in_specs=in_specs,
        out_specs=pl.BlockSpec((tq, d), lambda bb, j: (row(bb, j), 0)),
        out_shape=jax.ShapeDtypeStruct((t, d), F32),
        scratch_shapes=[
            pltpu.VMEM((wq, tq), BF16),
            pltpu.VMEM((SWA_KV_HEADS, tq + WINDOW, HEAD_DIM), BF16),
            pltpu.VMEM((wkv, tq + WINDOW), BF16),
            pltpu.VMEM((SWA_HEADS, 2 * WINDOW, WINDOW), F32),
            pltpu.VMEM((SWA_HEADS, 2 * WINDOW, WINDOW), BF16),
            pltpu.VMEM((wq, tq), BF16),
        ],
        compiler_params=_params(("parallel", "parallel")),
        name="swa_mix",
    )(sink_col, z, z, z, z, z, z, x2, km, vmt, wo)


def _ffn_kernel(x_ref, gain_ref, wgu_ref, wd_ref, out_ref, *, ff_chunk):
    x = x_ref[...]
    h = _rms(x, gain_ref[...]).astype(BF16)
    d_ff = wd_ref.shape[0]
    acc = x
    for c0 in range(0, d_ff, ff_chunk):
        c1 = min(c0 + ff_chunk, d_ff)
        gate = _dot(h, wgu_ref[:, c0:c1])
        up = _dot(h, wgu_ref[:, d_ff + c0:d_ff + c1])
        acc = acc + _dot((_silu(gate) * up).astype(BF16), wd_ref[c0:c1, :])
    out_ref[...] = acc


def _ffn(x2, gain, w_gu, w_down, tm):
    t, d = x2.shape
    d_ff = w_down.shape[0]
    resident = pl.Buffered(1)
    return pl.pallas_call(
        functools.partial(_ffn_kernel, ff_chunk=6 * MXU_WIDTH),
        grid=(t // tm,),
        in_specs=[
            pl.BlockSpec((tm, d), lambda i: (i, 0)),
            pl.BlockSpec((1, d), lambda i: (0, 0)),
            pl.BlockSpec((d, 2 * d_ff), lambda i: (0, 0), pipeline_mode=resident),
            pl.BlockSpec((d_ff, d), lambda i: (0, 0), pipeline_mode=resident),
        ],
        out_specs=pl.BlockSpec((tm, d), lambda i: (i, 0)),
        out_shape=jax.ShapeDtypeStruct((t, d), F32),
        compiler_params=_params(("parallel",)),
        name="ffn_dense",
    )(x2, gain.reshape(1, d), w_gu, w_down)


def _router_kernel(x_ref, gain_ref, rw_ref, h_ref, ri_ref, rg_ref, cnt_ref, carry_s):
    @pl.when(pl.program_id(0) == 0)
    def _():
        carry_s[...] = jnp.zeros_like(carry_s)

    tb = x_ref.shape[0]
    h = _rms(x_ref[...], gain_ref[...])
    _store_token_tiles(h_ref, h)
    rw = rw_ref[...]
    h_hi, rw_hi = h.astype(BF16), rw.astype(BF16)
    h_lo = (h - h_hi.astype(F32)).astype(BF16)
    rw_lo = (rw - rw_hi.astype(F32)).astype(BF16)
    logits = _dot(h_hi, rw_hi) + (_dot(h_lo, rw_hi) + _dot(h_hi, rw_lo))
    lane = lax.broadcasted_iota(jnp.int32, logits.shape, 1)
    logits = jnp.where(lane < N_EXPERTS, logits, -jnp.inf)
    m1 = jnp.max(logits, axis=-1, keepdims=True)
    i1 = jnp.min(jnp.where(logits == m1, lane, LANES), axis=-1, keepdims=True)
    rest = jnp.where(lane == i1, -jnp.inf, logits)
    m2 = jnp.max(rest, axis=-1, keepdims=True)
    i2 = jnp.min(jnp.where(rest == m2, lane, LANES), axis=-1, keepdims=True)
    e2 = jnp.exp(m2 - m1)
    g1 = 1.0 / (1.0 + e2)
    g2 = e2 / (1.0 + e2)

    sel1 = lane == i1
    sel2 = lane == i2
    onehot = jnp.where(jnp.logical_or(sel1, sel2), 1.0, 0.0)
    ti = lax.broadcasted_iota(jnp.int32, (tb, tb), 0)
    tj = lax.broadcasted_iota(jnp.int32, (tb, tb), 1)
    before = jnp.where(tj < ti, 1.0, 0.0).astype(BF16)
    seen = carry_s[...] + _dot(before, onehot.astype(BF16))
    rank1 = jnp.sum(jnp.where(sel1, seen, 0.0), axis=-1, keepdims=True).astype(jnp.int32)
    rank2 = jnp.sum(jnp.where(sel2, seen, 0.0), axis=-1, keepdims=True).astype(jnp.int32)
    carry_s[...] += jnp.sum(onehot, axis=0, keepdims=True)
    cnt_ref[...] = carry_s[...].astype(jnp.int32)

    ri = jnp.where(lane == 0, i1, jnp.where(lane == 1, i2, jnp.where(lane == 2, rank1, rank2)))
    rg = jnp.where(lane == 0, g1, g2)
    ri_ref[...] = ri[:, :8]
    rg_ref[...] = rg[:, :8]


def _router(x2, gain, rw, tb):
    t, d = x2.shape
    return pl.pallas_call(
        _router_kernel,
        grid=(t // tb,),
        in_specs=[
            pl.BlockSpec((tb, d), lambda i: (i, 0)),
            pl.BlockSpec((1, d), lambda i: (0, 0)),
            pl.BlockSpec((d, LANES), lambda i: (0, 0)),
        ],
        out_specs=[
            pl.BlockSpec((tb * TILE_ROWS, LANES), lambda i: (i, 0)),
            pl.BlockSpec((tb, 8), lambda i: (i, 0)),
            pl.BlockSpec((tb, 8), lambda i: (i, 0)),
            pl.BlockSpec((1, LANES), lambda i: (0, 0)),
        ],
        out_shape=[
            jax.ShapeDtypeStruct((t * TILE_ROWS, LANES), F32),
            jax.ShapeDtypeStruct((t, 8), jnp.int32),
            jax.ShapeDtypeStruct((t, 8), F32),
            jax.ShapeDtypeStruct((1, LANES), jnp.int32),
        ],
        scratch_shapes=[pltpu.VMEM((1, LANES), F32)],
        compiler_params=_params(("arbitrary",)),
        name="router",
    )(x2, gain.reshape(1, d), rw)


def _tile_rows(i):
    return pl.ds(pl.multiple_of(i * TILE_ROWS, TILE_ROWS), TILE_ROWS)


def _dispatch_kernel(pad_lo_ref, pad_hi_ref, dest_ref, src_ref, dst_ref, sem):
    gb = dest_ref.shape[2] // 2

    def issue(t, carry):
        tile = src_ref.at[_tile_rows(t)]
        for k in range(2):
            pltpu.make_async_copy(tile, dst_ref.at[_tile_rows(dest_ref[0, 0, 2 * t + k])], sem).start(priority=k)
        return carry

    lax.fori_loop(0, gb, issue, 0, unroll=ISSUE_UNROLL)
    for _ in range(2):
        pltpu.make_async_copy(src_ref, dst_ref.at[pl.ds(0, gb * TILE_ROWS)], sem).wait()

    @pl.when(pl.program_id(0) == pl.num_programs(0) - 1)
    def _():
        for e in range(N_EXPERTS):
            lo, hi = pad_lo_ref[e], pad_hi_ref[e]

            def fill(r, carry):
                pltpu.make_async_copy(src_ref.at[_tile_rows(0)], dst_ref.at[_tile_rows(r)], sem).start()
                return carry

            def drain(r, carry):
                pltpu.make_async_copy(src_ref.at[_tile_rows(0)], dst_ref.at[_tile_rows(r)], sem).wait()
                return carry

            lax.fori_loop(lo, hi, fill, 0)
            lax.fori_loop(lo, hi, drain, 0)


def _dispatch(src, dest, pad_lo, pad_hi, n_rows, gb):
    t = src.shape[0] // TILE_ROWS
    return pl.pallas_call(
        _dispatch_kernel,
        grid_spec=pltpu.PrefetchScalarGridSpec(
            num_scalar_prefetch=2,
            grid=(t // gb,),
            in_specs=[
                pl.BlockSpec((1, 1, 2 * gb), lambda i, lo, hi: (i, 0, 0), memory_space=pltpu.SMEM),
                pl.BlockSpec((gb * TILE_ROWS, LANES), lambda i, lo, hi: (i, 0)),
            ],
            out_specs=pl.BlockSpec(memory_space=pl.ANY),
            scratch_shapes=[pltpu.SemaphoreType.DMA(())],
        ),
        out_shape=jax.ShapeDtypeStruct((n_rows * TILE_ROWS, LANES), src.dtype),
        compiler_params=_params(("arbitrary",)),
        name="moe_dispatch",
    )(pad_lo, pad_hi, dest.reshape(t // gb, 1, 2 * gb), src)


def _moe_kernel(te_ref, nu_ref, xs_ref, wgu_ref, wd_ref, out_ref, *, ff_chunk):
    used = pl.program_id(0) < nu_ref[0]

    @pl.when(used)
    def _():
        tm = xs_ref.shape[0] // TILE_ROWS
        h = jnp.concatenate([part.astype(BF16) for part in _load_token_tiles(xs_ref, tm)], axis=1)
        d_ff = wd_ref.shape[1]
        acc = None
        for c0 in range(0, d_ff, ff_chunk):
            c1 = min(c0 + ff_chunk, d_ff)
            gate = _dot(h, wgu_ref[0, :, c0:c1])
            up = _dot(h, wgu_ref[0, :, d_ff + c0:d_ff + c1])
            part = _dot((_silu(gate) * up).astype(BF16), wd_ref[0, c0:c1, :])
            acc = part if acc is None else acc + part
        _store_token_tiles(out_ref, acc)

    @pl.when(jnp.logical_not(used))
    def _():
        out_ref[...] = jnp.zeros_like(out_ref)


def _moe(xs, tile_expert, n_used, w_gu, w_down, tm):
    n_rows = xs.shape[0] // TILE_ROWS
    d = w_down.shape[2]
    d_ff = w_down.shape[1]
    nt = n_rows // tm
    resident = pl.Buffered(1)
    return pl.pallas_call(
        functools.partial(_moe_kernel, ff_chunk=7 * MXU_WIDTH),
        grid_spec=pltpu.PrefetchScalarGridSpec(
            num_scalar_prefetch=2,
            grid=(nt,),
            in_specs=[
                pl.BlockSpec((tm * TILE_ROWS, LANES), lambda i, te, nu: (jnp.minimum(i, nu[0] - 1), 0)),
                pl.BlockSpec((1, d, 2 * d_ff), lambda i, te, nu: (te[i], 0, 0), pipeline_mode=resident),
                pl.BlockSpec((1, d_ff, d), lambda i, te, nu: (te[i], 0, 0), pipeline_mode=resident),
            ],
            out_specs=pl.BlockSpec((tm * TILE_ROWS, LANES), lambda i, te, nu: (i, 0)),
        ),
        out_shape=jax.ShapeDtypeStruct((n_rows * TILE_ROWS, LANES), F32),
        compiler_params=_params(("arbitrary",)),
        name="moe_experts",
    )(tile_expert, n_used, xs, w_gu, w_down)


def _combine_kernel(dest_ref, next_ref, x_ref, rg_ref, gain_ref, ys_ref, out_ref, buf, sem):
    tb = x_ref.shape[0]
    i = pl.program_id(0)
    n = pl.num_programs(0)
    slot = i % 2

    def gather(idx_ref, s):
        def issue(t, carry):
            for k in range(2):
                pltpu.make_async_copy(ys_ref.at[_tile_rows(idx_ref[0, 0, 2 * t + k])],
                                      buf.at[s, k, _tile_rows(t)], sem.at[s]).start(priority=k)
            return carry

        lax.fori_loop(0, tb, issue, 0, unroll=ISSUE_UNROLL)

    @pl.when(i == 0)
    def _():
        gather(dest_ref, 0)

    @pl.when(i + 1 < n)
    def _():
        gather(next_ref, 1 - slot)

    for k in range(2):
        pltpu.make_async_copy(ys_ref.at[pl.ds(0, tb * TILE_ROWS)], buf.at[slot, k], sem.at[slot]).wait()

    g = rg_ref[...]
    g1, g2 = g[:, 0:1], g[:, 1:2]
    ya = _load_token_tiles(buf.at[slot, 0], tb)
    yb = _load_token_tiles(buf.at[slot, 1], tb)
    y = jnp.concatenate([g1 * a + g2 * b for a, b in zip(ya, yb)], axis=1)
    out_ref[...] = _rms(x_ref[...] + y, gain_ref[...])


def _combine(x2, ys, dest, rg, gain, tb):
    t, d = x2.shape
    n = t // tb
    dest3 = dest.reshape(n, 1, 2 * tb)
    return pl.pallas_call(
        _combine_kernel,
        grid=(n,),
        in_specs=[
            pl.BlockSpec((1, 1, 2 * tb), lambda i: (i, 0, 0), memory_space=pltpu.SMEM),
            pl.BlockSpec((1, 1, 2 * tb), lambda i: (jnp.minimum(i + 1, n - 1), 0, 0), memory_space=pltpu.SMEM),
            pl.BlockSpec((tb, d), lambda i: (i, 0)),
            pl.BlockSpec((tb, 8), lambda i: (i, 0)),
            pl.BlockSpec((1, d), lambda i: (0, 0)),
            pl.BlockSpec(memory_space=pl.ANY),
        ],
        out_specs=pl.BlockSpec((tb, d), lambda i: (i, 0)),
        out_shape=jax.ShapeDtypeStruct((t, d), F32),
        scratch_shapes=[pltpu.VMEM((2, 2, tb * TILE_ROWS, LANES), F32), pltpu.SemaphoreType.DMA((2,))],
        compiler_params=_params(("arbitrary",)),
        name="moe_combine",
    )(dest3, dest3, x2, rg, gain.reshape(1, d), ys)


def _pad_heads(w, heads, width, padded):
    lead = w.shape[:-1]
    w = w.reshape(lead + (heads, width))
    w = jnp.pad(w, [(0, 0)] * len(lead) + [(0, 0), (0, padded - width)])
    return w.reshape(lead + (heads * padded,))


def _layer_a_weights(w_in, gate_w, gate_b, out_norm, w_out):
    dqk = GLA_HEADS * GLA_DK
    dv = GLA_HEADS * GLA_DV
    o = 0
    wq = w_in[:, o:o + dqk]; o += dqk
    wk = w_in[:, o:o + dqk]; o += dqk
    wv = w_in[:, o:o + dv]; o += dv
    wl = w_in[:, o:o + GLA_LOWRANK]; o += GLA_LOWRANK
    wr = w_in[:, o:o + dv]; o += dv
    wm = w_in[:, o:o + MEM_WIDTH]
    w = jnp.concatenate([
        _pad_heads(wq, GLA_HEADS, GLA_DK, GLA_DK_PAD),
        _pad_heads(wk, GLA_HEADS, GLA_DK, GLA_DK_PAD),
        _pad_heads(wv, GLA_HEADS, GLA_DV, GLA_DV_PAD),
        _pad_heads(wr, GLA_HEADS, GLA_DV, GLA_DV_PAD),
        wm,
        jnp.pad(wl, ((0, 0), (0, LANES - GLA_LOWRANK))),
    ], axis=1).astype(BF16)
    gw = _pad_heads(gate_w, GLA_HEADS, GLA_DK, GLA_DK_PAD)
    gw = jnp.pad(gw, ((0, LANES - GLA_LOWRANK), (0, 0))).astype(BF16)
    gb = _pad_heads(gate_b, GLA_HEADS, GLA_DK, GLA_DK_PAD).reshape(1, -1)
    og = jnp.pad(out_norm, (0, GLA_DV_PAD - GLA_DV)).reshape(1, -1)
    wo_main = w_out[:dv].reshape(GLA_HEADS, GLA_DV, -1)
    wo_main = jnp.pad(wo_main, ((0, 0), (0, GLA_DV_PAD - GLA_DV), (0, 0))).reshape(GLA_HEADS * GLA_DV_PAD, -1)
    wo = jnp.concatenate([wo_main, w_out[dv:]], axis=0).astype(BF16)
    return w, gw, gb, og, wo


def kernel(x, mem, positions, mix_norm, w_in_a, gla_gate_w, gla_gate_b, gla_out_norm, w_in_b, swa_sinks,
           mem_norm, w_mem_kv, w_out, ffn_norm, ffn_w_gate_up, ffn_w_down, router_w, exp_w_gate_up,
           exp_w_down, final_norm):
    b, s, d = x.shape
    t = b * s
    x2 = x.reshape(t, d)
    assert d == TILE_ROWS * LANES
    tq = min(512, s)
    tm_moe = 512
    gb = min(1024, t)

    km, vm, vmt = _memkv(mem, mem_norm, w_mem_kv)

    wa, gw, gbias, og, wo_a = _layer_a_weights(w_in_a[0], gla_gate_w[0], gla_gate_b[0], gla_out_norm[0], w_out[0])
    z = _proj(x2, mix_norm[0], wa, tq)
    x2 = _gla_mix(z, x2, gw, gbias, og, km[0], vm[0], wo_a, b, s, tq)
    x2 = _ffn(x2, ffn_norm[0], ffn_w_gate_up[0].astype(BF16), ffn_w_down[0].astype(BF16), tq)

    n_q = SWA_HEADS * HEAD_DIM
    n_kv = SWA_KV_HEADS * HEAD_DIM
    wb = jnp.concatenate([w_in_b[0][:, :n_q] * (HEAD_DIM ** -0.5), w_in_b[0][:, n_q:]], axis=1).astype(BF16)
    half = ROPE_DIM // 2
    inv_freq = jnp.power(jnp.float32(ROPE_THETA), -jnp.arange(half, dtype=F32) / half)
    invf = jnp.tile(inv_freq, LANES // half).reshape(1, LANES)
    pos = jnp.repeat(positions.astype(F32).reshape(t), half).reshape(t * half // LANES, LANES)
    z = _proj(x2, mix_norm[1], wb, tq, rope=(pos, invf, n_q + n_kv))
    x2 = _swa_mix(z, x2, swa_sinks[0], km[1], vmt[1], w_out[1].astype(BF16), b, s, tq)

    rw = jnp.pad(router_w[0], ((0, 0), (0, LANES - N_EXPERTS)))
    h, ri, rg, counts = _router(x2, ffn_norm[1], rw, tq)

    cnt = counts[0, :N_EXPERTS]
    padded = ((cnt + tm_moe - 1) // tm_moe) * tm_moe
    ends = jnp.cumsum(padded)
    offs = ends - padded
    n_tiles = 2 * t // tm_moe + N_EXPERTS
    n_used = (ends[-1] // tm_moe).astype(jnp.int32)
    tile_start = jnp.arange(n_tiles, dtype=jnp.int32) * tm_moe
    tile_expert = jnp.sum(tile_start[:, None] >= ends[None, :], axis=1).astype(jnp.int32)
    tile_expert = jnp.minimum(tile_expert, N_EXPERTS - 1)
    last_expert = tile_expert[jnp.maximum(n_used - 1, 0)]
    tile_expert = jnp.where(jnp.arange(n_tiles) < n_used, tile_expert, last_expert)
    experts = ri[:, 0:2]
    off_of = jnp.sum(jnp.where(experts[:, :, None] == jnp.arange(N_EXPERTS)[None, None, :],
                               offs[None, None, :], 0), axis=-1)
    dest = (off_of + ri[:, 2:4]).astype(jnp.int32)

    pad_hi = ends.at[N_EXPERTS - 1].set(n_tiles * tm_moe)
    xs = _dispatch(h, dest, (offs + cnt).astype(jnp.int32), pad_hi.astype(jnp.int32), n_tiles * tm_moe, gb)
    ys = _moe(xs, tile_expert, n_used.reshape(1), exp_w_gate_up[0].astype(BF16), exp_w_down[0].astype(BF16),
              tm_moe)
    out = _combine(x2, ys, dest, rg, final_norm, gb)
    return out.reshape(b, s, d)
```

```python
import functools

import jax
import jax.numpy as jnp
from jax import lax
from jax.experimental import pallas as pl
from jax.experimental.pallas import tpu as pltpu

F32 = jnp.float32
BF16 = jnp.bfloat16

HEAD_DIM = 64
MEM_HEADS = 4
MEM_WIDTH = MEM_HEADS * HEAD_DIM
GLA_HEADS = 4
GLA_DK = 96
GLA_DV = 192
GLA_LOWRANK = 16
GLA_TAU = 16.0
GLA_CHUNK = 64
SWA_HEADS = 12
SWA_KV_HEADS = 4
WINDOW = 128
ROPE_DIM = 16
ROPE_THETA = 500000.0
N_EXPERTS = 8
EPS = 1e-6

LANES = 128
TILE_ROWS = 8
GLA_DK_PAD = 128
GLA_DV_PAD = 256
VMEM_LIMIT = 56 * 1024 * 1024
ISSUE_UNROLL = 4
MXU_WIDTH = 256

NT_DIMS = (((1,), (1,)), ((), ()))
TN_DIMS = (((0,), (0,)), ((), ()))


def _params(sem):
    return pltpu.CompilerParams(dimension_semantics=sem, vmem_limit_bytes=VMEM_LIMIT)


def _rms(xf, gain):
    return xf * lax.rsqrt(jnp.mean(xf * xf, axis=-1, keepdims=True) + EPS) * gain


def _dot(a, b):
    return jnp.dot(a, b, preferred_element_type=F32)


def _silu(x):
    return x * jax.nn.sigmoid(x)


def _store_token_tiles(ref, val):
    n = val.shape[0]
    for c in range(TILE_ROWS):
        ref[pl.ds(c, n, stride=TILE_ROWS), :] = val[:, c * LANES:(c + 1) * LANES].astype(ref.dtype)


def _load_token_tiles(ref, n):
    return [ref[pl.ds(c, n, stride=TILE_ROWS), :] for c in range(TILE_ROWS)]


def _memkv_kernel(mem_ref, gain_ref, w_ref, km_ref, vm_ref, vmt_ref):
    mem_n = _rms(mem_ref[0], gain_ref[0]).astype(BF16)
    kv = _dot(mem_n, w_ref[0].astype(BF16))
    k = kv[:, :MEM_WIDTH] * (HEAD_DIM ** -0.5)
    v = kv[:, MEM_WIDTH:]
    head = lax.broadcasted_iota(jnp.int32, k.shape, 1) // HEAD_DIM
    for h in range(MEM_HEADS):
        km_ref[0, 0, h] = jnp.where(head == h, k, 0.0).astype(BF16)
        v_h = jnp.where(head == h, v, 0.0)
        vm_ref[0, 0, h] = v_h.astype(BF16)
        vmt_ref[0, 0, h] = jnp.transpose(v_h).astype(BF16)


def _memkv(mem, mem_norm, w_mem_kv):
    depth = mem_norm.shape[0]
    b, n_mem, d = mem.shape
    out = jax.ShapeDtypeStruct((depth, b, MEM_HEADS, n_mem, MEM_WIDTH), BF16)
    out_t = jax.ShapeDtypeStruct((depth, b, MEM_HEADS, MEM_WIDTH, n_mem), BF16)
    blk = pl.BlockSpec((1, 1, MEM_HEADS, n_mem, MEM_WIDTH), lambda i, bb: (i, bb, 0, 0, 0))
    blk_t = pl.BlockSpec((1, 1, MEM_HEADS, MEM_WIDTH, n_mem), lambda i, bb: (i, bb, 0, 0, 0))
    return pl.pallas_call(
        _memkv_kernel,
        grid=(depth, b),
        in_specs=[
            pl.BlockSpec((1, n_mem, d), lambda i, bb: (bb, 0, 0)),
            pl.BlockSpec((1, 1, d), lambda i, bb: (i, 0, 0)),
            pl.BlockSpec((1, d, 2 * MEM_WIDTH), lambda i, bb: (i, 0, 0)),
        ],
        out_specs=[blk, blk, blk_t],
        out_shape=[out, out, out_t],
        compiler_params=_params(("arbitrary", "arbitrary")),
        name="mem_kv",
    )(mem, mem_norm.reshape(depth, 1, d), w_mem_kv)


def _mem_attention(qm, km_ref, vm_ref):
    acc = jnp.zeros((qm.shape[0], MEM_WIDTH), F32)
    for h in range(MEM_HEADS):
        s = lax.dot_general(qm, km_ref[h], NT_DIMS, preferred_element_type=F32)
        p = jnp.exp(s - jnp.max(s, axis=-1, keepdims=True))
        p = p / jnp.sum(p, axis=-1, keepdims=True)
        acc = acc + _dot(p.astype(BF16), vm_ref[h])
    return acc


def _mem_attention_t(qm, km_ref, vmt_ref):
    qm_t = jnp.transpose(qm.astype(F32)).astype(BF16)
    acc = jnp.zeros(qm_t.shape, F32)
    for h in range(MEM_HEADS):
        s = _dot(km_ref[h], qm_t)
        p = jnp.exp(s - jnp.max(s, axis=0, keepdims=True))
        inv = 1.0 / jnp.sum(p, axis=0, keepdims=True)
        acc = acc + _dot(vmt_ref[h], p.astype(BF16)) * inv
    return acc


def _proj_kernel(x_ref, gain_ref, w_ref, z_ref, *, col_chunk):
    h = _rms(x_ref[...], gain_ref[...]).astype(BF16)
    n = w_ref.shape[1]
    for c0 in range(0, n, col_chunk):
        c1 = min(c0 + col_chunk, n)
        z_ref[:, c0:c1] = _dot(h, w_ref[:, c0:c1]).astype(BF16)


def _proj_rope_kernel(x_ref, gain_ref, w_ref, pos_ref, invf_ref, rep_ref, pick_ref, spread_ref, z_ref, *,
                      rope_cols):
    h = _rms(x_ref[...], gain_ref[...]).astype(BF16)
    tm = x_ref.shape[0]
    half = ROPE_DIM // 2
    ang = pos_ref[...] * invf_ref[...]
    cs = jnp.concatenate([jnp.cos(ang), jnp.sin(ang)], axis=1)
    c1 = cs.astype(BF16)
    r1 = cs - c1.astype(F32)
    c2 = r1.astype(BF16)
    c3 = (r1 - c2.astype(F32)).astype(BF16)
    full = jnp.zeros((tm, 2 * LANES), F32)
    for piece in (c1, c2, c3):
        rows = _dot(rep_ref[...], piece) * pick_ref[...]
        full = full + _dot(rows.astype(BF16), spread_ref[...])
    lane = lax.broadcasted_iota(jnp.int32, (tm, LANES), 1) % HEAD_DIM
    cosf = jnp.where(lane < ROPE_DIM, full[:, :LANES], 1.0)
    sn = full[:, LANES:]
    sinf = jnp.where(lane < half, -sn, sn)
    first = lane < half
    n = w_ref.shape[1]
    wide = 2 * LANES
    for c0 in range(0, n, wide):
        zw = _dot(h, w_ref[:, c0:c0 + wide])
        for s0 in range(0, wide, LANES):
            zc = zw[:, s0:s0 + LANES]
            if c0 + s0 < rope_cols:
                partner = jnp.where(first, pltpu.roll(zc, LANES - half, 1), pltpu.roll(zc, half, 1))
                zc = zc * cosf + partner * sinf
            z_ref[:, c0 + s0:c0 + s0 + LANES] = zc.astype(BF16)


def _proj(x2, gain, w, tm, rope=None):
    t, d = x2.shape
    n = w.shape[1]
    in_specs = [
        pl.BlockSpec((tm, d), lambda i: (i, 0)),
        pl.BlockSpec((1, d), lambda i: (0, 0)),
        pl.BlockSpec((d, n), lambda i: (0, 0)),
    ]
    args = [x2, gain.reshape(1, d), w]
    if rope is None:
        kern = functools.partial(_proj_kernel, col_chunk=4 * MXU_WIDTH)
    else:
        pos, invf, rope_cols = rope
        kern = functools.partial(_proj_rope_kernel, rope_cols=rope_cols)
        half = ROPE_DIM // 2
        per_row = LANES // half
        ti = jnp.arange(tm, dtype=jnp.int32)[:, None]
        li = jnp.arange(LANES, dtype=jnp.int32)[None, :]
        rep = (jnp.arange(tm // per_row, dtype=jnp.int32)[None, :] == ti // per_row).astype(BF16)
        pick = jnp.tile((li // half == ti % per_row).astype(F32), (1, 2))
        spread = jnp.logical_and(li.T % half == li % half, li % HEAD_DIM < ROPE_DIM).astype(BF16)
        zero = jnp.zeros_like(spread)
        spread = jnp.block([[spread, zero], [zero, spread]])
        in_specs += [
            pl.BlockSpec((tm // per_row, LANES), lambda i: (i, 0)),
            pl.BlockSpec((1, LANES), lambda i: (0, 0)),
            pl.BlockSpec(rep.shape, lambda i: (0, 0)),
            pl.BlockSpec(pick.shape, lambda i: (0, 0)),
            pl.BlockSpec(spread.shape, lambda i: (0, 0)),
        ]
        args += [pos, invf, rep, pick, spread]
    return pl.pallas_call(
        kern,
        grid=(t // tm,),
        in_specs=in_specs,
        out_specs=pl.BlockSpec((tm, n), lambda i: (i, 0)),
        out_shape=jax.ShapeDtypeStruct((t, n), BF16),
        compiler_params=_params(("parallel",)),
        name="in_proj" if rope is None else "in_proj_rope",
    )(*args)


def _gla_mix_kernel(q_ref, k_ref, v_ref, r_ref, qm_ref, gl_ref, x_ref, gw_ref, gb_ref, on_ref, tri_ref,
                    km_ref, vm_ref, wo_ref, out_ref, qd_s, kd_s, ke_s, dec_s, o_s, sc_s, main_s, st_s):
    @pl.when(pl.program_id(1) == 0)
    def _():
        st_s[...] = jnp.zeros_like(st_s)

    tq = q_ref.shape[0]
    c_len = GLA_CHUNK
    n_chunks = tq // c_len
    logits = _dot(gl_ref[...], gw_ref[...]) + gb_ref[...]
    log_sig = jnp.minimum(logits, 0.0) - jnp.log(1.0 + jnp.exp(-jnp.abs(logits)))
    lane = lax.broadcasted_iota(jnp.int32, logits.shape, 1) % GLA_DK_PAD
    la = jnp.where(lane < GLA_DK, log_sig * (1.0 / GLA_TAU), 0.0)

    la_hi = la.astype(BF16)
    la_lo = (la - la_hi.astype(F32)).astype(BF16)
    tri = tri_ref[...]
    sbr = tri.shape[0]
    b = jnp.concatenate([_dot(tri, la_hi[s0:s0 + sbr]) + _dot(tri, la_lo[s0:s0 + sbr])
                         for s0 in range(0, tq, sbr)], axis=0)
    b3 = b.reshape(n_chunks, c_len, b.shape[1])
    b_last = jnp.broadcast_to(b3[:, c_len - 1:c_len, :], b3.shape).reshape(b.shape)
    qf = q_ref[...].astype(F32)
    kf = k_ref[...].astype(F32)
    qd_s[...] = (qf * (GLA_DK ** -0.5) * jnp.exp(b)).astype(BF16)
    kd_s[...] = (kf * jnp.exp(-b)).astype(BF16)
    ke_s[...] = (kf * jnp.exp(b_last - b)).astype(BF16)
    dec_s[...] = jnp.exp(b_last)

    kcols = [slice(h * GLA_DK_PAD, (h + 1) * GLA_DK_PAD) for h in range(GLA_HEADS)]
    vcols = [slice(h * GLA_DV_PAD, (h + 1) * GLA_DV_PAD) for h in range(GLA_HEADS)]

    sb = min(4 * c_len, tq)
    ri = lax.broadcasted_iota(jnp.int32, (sb, sb), 0)
    ci = lax.broadcasted_iota(jnp.int32, (sb, sb), 1)
    intra = jnp.logical_and(ri // c_len == ci // c_len, ci <= ri)
    for s0 in range(0, tq, sb):
        rows = slice(s0, s0 + sb)
        for h in range(GLA_HEADS):
            attn = lax.dot_general(qd_s[rows, kcols[h]], kd_s[rows, kcols[h]], NT_DIMS,
                                   preferred_element_type=F32)
            attn = jnp.where(intra, attn, 0.0).astype(BF16)
            o_s[rows, vcols[h]] = _dot(attn, v_ref[rows, vcols[h]])

    for c in range(n_chunks):
        rows = slice(c * c_len, (c + 1) * c_len)
        for h in range(GLA_HEADS):
            st = st_s[h]
            sc_s[c * GLA_HEADS + h] = st.astype(BF16)
            kv = lax.dot_general(v_ref[rows, vcols[h]], ke_s[rows, kcols[h]], TN_DIMS, preferred_element_type=F32)
            st_s[h] = st * dec_s[c * c_len:c * c_len + 1, kcols[h]] + kv
    for c in range(n_chunks):
        rows = slice(c * c_len, (c + 1) * c_len)
        for h in range(GLA_HEADS):
            o_s[rows, vcols[h]] += lax.dot_general(qd_s[rows, kcols[h]], sc_s[c * GLA_HEADS + h], NT_DIMS,
                                                   preferred_element_type=F32)

    for h in range(GLA_HEADS):
        o = o_s[:, vcols[h]]
        ms = jnp.sum(o * o, axis=-1, keepdims=True) * (1.0 / GLA_DV)
        o_n = o * lax.rsqrt(ms + EPS) * on_ref[...]
        main_s[:, vcols[h]] = (o_n * _silu(r_ref[:, vcols[h]].astype(F32))).astype(BF16)

    mem_out = _mem_attention(qm_ref[...], km_ref, vm_ref)
    n_main = GLA_HEADS * GLA_DV_PAD
    mixed = _dot(main_s[...], wo_ref[0:n_main, :]) + _dot(mem_out.astype(BF16), wo_ref[n_main:, :])
    out_ref[...] = x_ref[...] + mixed


def _gla_mix(z, x2, gate_w, gate_b, out_gain, km, vm, wo, b, s, tq):
    t, d = x2.shape
    nj = s // tq
    wq = GLA_HEADS * GLA_DK_PAD
    wv = GLA_HEADS * GLA_DV_PAD
    row = lambda bb, j: bb * nj + j
    qm_blk = (2 * wq + 2 * wv) // MEM_WIDTH
    gl_blk = (2 * wq + 2 * wv + MEM_WIDTH) // LANES
    n_mem = km.shape[2]
    sbr = min(MXU_WIDTH, tq)
    in_specs = [
        pl.BlockSpec((tq, wq), lambda bb, j: (row(bb, j), 0)),
        pl.BlockSpec((tq, wq), lambda bb, j: (row(bb, j), 1)),
        pl.BlockSpec((tq, wv), lambda bb, j: (row(bb, j), 1)),
        pl.BlockSpec((tq, wv), lambda bb, j: (row(bb, j), 2)),
        pl.BlockSpec((tq, MEM_WIDTH), lambda bb, j: (row(bb, j), qm_blk)),
        pl.BlockSpec((tq, LANES), lambda bb, j: (row(bb, j), gl_blk)),
        pl.BlockSpec((tq, d), lambda bb, j: (row(bb, j), 0)),
        pl.BlockSpec(gate_w.shape, lambda bb, j: (0, 0)),
        pl.BlockSpec(gate_b.shape, lambda bb, j: (0, 0)),
        pl.BlockSpec(out_gain.shape, lambda bb, j: (0, 0)),
        pl.BlockSpec((sbr, sbr), lambda bb, j: (0, 0)),
        pl.BlockSpec((None, MEM_HEADS, n_mem, MEM_WIDTH), lambda bb, j: (bb, 0, 0, 0)),
        pl.BlockSpec((None, MEM_HEADS, n_mem, MEM_WIDTH), lambda bb, j: (bb, 0, 0, 0)),
        pl.BlockSpec(wo.shape, lambda bb, j: (0, 0)),
    ]
    idx = jnp.arange(sbr, dtype=jnp.int32)
    same = idx[:, None] // GLA_CHUNK == idx[None, :] // GLA_CHUNK
    tri = jnp.logical_and(same, idx[None, :] <= idx[:, None]).astype(BF16)
    n_chunks = tq // GLA_CHUNK
    return pl.pallas_call(
        _gla_mix_kernel,
        grid=(b, nj),
        in_specs=in_specs,
        out_specs=pl.BlockSpec((tq, d), lambda bb, j: (row(bb, j), 0)),
        out_shape=jax.ShapeDtypeStruct((t, d), F32),
        scratch_shapes=[
            pltpu.VMEM((tq, wq), BF16),
            pltpu.VMEM((tq, wq), BF16),
            pltpu.VMEM((tq, wq), BF16),
            pltpu.VMEM((tq, wq), F32),
            pltpu.VMEM((tq, wv), F32),
            pltpu.VMEM((n_chunks * GLA_HEADS, GLA_DV_PAD, GLA_DK_PAD), BF16),
            pltpu.VMEM((tq, wv), BF16),
            pltpu.VMEM((GLA_HEADS, GLA_DV_PAD, GLA_DK_PAD), F32),
        ],
        compiler_params=_params(("arbitrary", "arbitrary")),
        name="gla_mix",
    )(z, z, z, z, z, z, x2, gate_w, gate_b, out_gain, tri, km, vm, wo)


def _swa_mix_kernel(sink_ref, q_ref, k_ref, v_ref, kp_ref, vp_ref, qm_ref, x_ref, km_ref, vmt_ref,
                    wo_ref, out_ref, qt_s, kg_s, vt_s, s_s, p_s, maint_s):
    has_prev = pl.program_id(1) > 0
    tq = q_ref.shape[0]
    w = WINDOW
    group = SWA_HEADS // SWA_KV_HEADS
    qt_s[...] = jnp.transpose(q_ref[...].astype(F32)).astype(BF16)
    for g in range(SWA_KV_HEADS):
        gc = slice(g * HEAD_DIM, (g + 1) * HEAD_DIM)
        kg_s[g, 0:w, :] = kp_ref[:, gc]
        kg_s[g, w:, :] = k_ref[:, gc]
    vt_s[:, 0:w] = jnp.transpose(vp_ref[...].astype(F32)).astype(BF16)
    vt_s[:, w:] = jnp.transpose(v_ref[...].astype(F32)).astype(BF16)
    kj = lax.broadcasted_iota(jnp.int32, (2 * w, w), 0)
    qi = lax.broadcasted_iota(jnp.int32, (2 * w, w), 1)
    band = jnp.logical_and(kj > qi, kj <= qi + w)
    bias = jnp.where(band, 0.0, -jnp.inf)
    bias_first = jnp.where(jnp.logical_and(band, jnp.logical_or(kj >= w, has_prev)), 0.0, -jnp.inf)
    sink = sink_ref[...]
    for sub in range(tq // w):
        r0 = sub * w
        sub_bias = bias_first if sub == 0 else bias
        for h in range(SWA_HEADS):
            s_s[h] = _dot(kg_s[h // group, r0:r0 + 2 * w, :],
                          qt_s[h * HEAD_DIM:(h + 1) * HEAD_DIM, r0:r0 + w]) + sub_bias
        s = s_s[...]
        m = jnp.maximum(jnp.max(s, axis=1, keepdims=True), sink)
        p = jnp.exp(s - m)
        inv = 1.0 / (jnp.sum(p, axis=1, keepdims=True) + jnp.exp(sink - m))
        p_s[...] = p.astype(BF16)
        for h in range(SWA_HEADS):
            g = h // group
            o_t = _dot(vt_s[g * HEAD_DIM:(g + 1) * HEAD_DIM, r0:r0 + 2 * w], p_s[h])
            maint_s[h * HEAD_DIM:(h + 1) * HEAD_DIM, r0:r0 + w] = (o_t * inv[h]).astype(BF16)

    mem_out_t = _mem_attention_t(qm_ref[...], km_ref, vmt_ref)
    n_main = SWA_HEADS * HEAD_DIM
    mixed = (lax.dot_general(maint_s[...], wo_ref[0:n_main, :], TN_DIMS, preferred_element_type=F32)
             + lax.dot_general(mem_out_t.astype(BF16), wo_ref[n_main:, :], TN_DIMS, preferred_element_type=F32))
    out_ref[...] = x_ref[...] + mixed


def _swa_mix(z, x2, sinks, km, vmt, wo, b, s, tq):
    t, d = x2.shape
    nj = s // tq
    wq = SWA_HEADS * HEAD_DIM
    wkv = SWA_KV_HEADS * HEAD_DIM
    row = lambda bb, j: bb * nj + j
    per = tq // WINDOW
    prev = lambda bb, j: jnp.maximum(row(bb, j) * per - 1, 0)
    k_blk = wq // wkv
    n_mem = km.shape[2]
    sink_col = jnp.broadcast_to(sinks.astype(F32)[:, None, None], (SWA_HEADS, 1, WINDOW))
    in_specs = [
        pl.BlockSpec(sink_col.shape, lambda bb, j: (0, 0, 0)),
        pl.BlockSpec((tq, wq), lambda bb, j: (row(bb, j), 0)),
        pl.BlockSpec((tq, wkv), lambda bb, j: (row(bb, j), k_blk)),
        pl.BlockSpec((tq, wkv), lambda bb, j: (row(bb, j), k_blk + 1)),
        pl.BlockSpec((WINDOW, wkv), lambda bb, j: (prev(bb, j), k_blk)),
        pl.BlockSpec((WINDOW, wkv), lambda bb, j: (prev(bb, j), k_blk + 1)),
        pl.BlockSpec((tq, MEM_WIDTH), lambda bb, j: (row(bb, j), k_blk + 2)),
        pl.BlockSpec((tq, d), lambda bb, j: (row(bb, j), 0)),
        pl.BlockSpec((None, MEM_HEADS, n_mem, MEM_WIDTH), lambda bb, j: (bb, 0, 0, 0)),
        pl.BlockSpec((None, MEM_HEADS, MEM_WIDTH, n_mem), lambda bb, j: (bb, 0, 0, 0)),
        pl.BlockSpec(wo.shape, lambda bb, j: (0, 0)),
    ]
    return pl.pallas_call(
        _swa_mix_kernel,
        grid=(b, nj),
        in_specs=in_specs,
        out_specs=pl.BlockSpec((tq, d), lambda bb, j: (row(bb, j), 0)),
        out_shape=jax.ShapeDtypeStruct((t, d), F32),
        scratch_shapes=[
            pltpu.VMEM((wq, tq), BF16),
            pltpu.VMEM((SWA_KV_HEADS, tq + WINDOW, HEAD_DIM), BF16),
            pltpu.VMEM((wkv, tq + WINDOW), BF16),
            pltpu.VMEM((SWA_HEADS, 2 * WINDOW, WINDOW), F32),
            pltpu.VMEM((SWA_HEADS, 2 * WINDOW, WINDOW), BF16),
            pltpu.VMEM((wq, tq), BF16),
        ],
        compiler_params=_params(("parallel", "parallel")),
        name="swa_mix",
    )(sink_col, z, z, z, z, z, z, x2, km, vmt, wo)


def _ffn_kernel(x_ref, gain_ref, wgu_ref, wd_ref, out_ref, *, ff_chunk):
    x = x_ref[...]
    h = _rms(x, gain_ref[...]).astype(BF16)
    d_ff = wd_ref.shape[0]
    acc = x
    for c0 in range(0, d_ff, ff_chunk):
        c1 = min(c0 + ff_chunk, d_ff)
        gate = _dot(h, wgu_ref[:, c0:c1])
        up = _dot(h, wgu_ref[:, d_ff + c0:d_ff + c1])
        acc = acc + _dot((_silu(gate) * up).astype(BF16), wd_ref[c0:c1, :])
    out_ref[...] = acc


def _ffn(x2, gain, w_gu, w_down, tm):
    t, d = x2.shape
    d_ff = w_down.shape[0]
    resident = pl.Buffered(1)
    return pl.pallas_call(
        functools.partial(_ffn_kernel, ff_chunk=6 * MXU_WIDTH),
        grid=(t // tm,),
        in_specs=[
            pl.BlockSpec((tm, d), lambda i: (i, 0)),
            pl.BlockSpec((1, d), lambda i: (0, 0)),
            pl.BlockSpec((d, 2 * d_ff), lambda i: (0, 0), pipeline_mode=resident),
            pl.BlockSpec((d_ff, d), lambda i: (0, 0), pipeline_mode=resident),
        ],
        out_specs=pl.BlockSpec((tm, d), lambda i: (i, 0)),
        out_shape=jax.ShapeDtypeStruct((t, d), F32),
        compiler_params=_params(("parallel",)),
        name="ffn_dense",
    )(x2, gain.reshape(1, d), w_gu, w_down)


def _router_kernel(x_ref, gain_ref, rw_ref, h_ref, ri_ref, rg_ref, cnt_ref, carry_s):
    @pl.when(pl.program_id(0) == 0)
    def _():
        carry_s[...] = jnp.zeros_like(carry_s)

    tb = x_ref.shape[0]
    h = _rms(x_ref[...], gain_ref[...])
    _store_token_tiles(h_ref, h)
    rw = rw_ref[...]
    h_hi, rw_hi = h.astype(BF16), rw.astype(BF16)
    h_lo = (h - h_hi.astype(F32)).astype(BF16)
    rw_lo = (rw - rw_hi.astype(F32)).astype(BF16)
    logits = _dot(h_hi, rw_hi) + (_dot(h_lo, rw_hi) + _dot(h_hi, rw_lo))
    lane = lax.broadcasted_iota(jnp.int32, logits.shape, 1)
    logits = jnp.where(lane < N_EXPERTS, logits, -jnp.inf)
    m1 = jnp.max(logits, axis=-1, keepdims=True)
    i1 = jnp.min(jnp.where(logits == m1, lane, LANES), axis=-1, keepdims=True)
    rest = jnp.where(lane == i1, -jnp.inf, logits)
    m2 = jnp.max(rest, axis=-1, keepdims=True)
    i2 = jnp.min(jnp.where(rest == m2, lane, LANES), axis=-1, keepdims=True)
    e2 = jnp.exp(m2 - m1)
    g1 = 1.0 / (1.0 + e2)
    g2 = e2 / (1.0 + e2)

    sel1 = lane == i1
    sel2 = lane == i2
    onehot = jnp.where(jnp.logical_or(sel1, sel2), 1.0, 0.0)
    ti = lax.broadcasted_iota(jnp.int32, (tb, tb), 0)
    tj = lax.broadcasted_iota(jnp.int32, (tb, tb), 1)
    before = jnp.where(tj < ti, 1.0, 0.0).astype(BF16)
    seen = carry_s[...] + _dot(before, onehot.astype(BF16))
    rank1 = jnp.sum(jnp.where(sel1, seen, 0.0), axis=-1, keepdims=True).astype(jnp.int32)
    rank2 = jnp.sum(jnp.where(sel2, seen, 0.0), axis=-1, keepdims=True).astype(jnp.int32)
    carry_s[...] += jnp.sum(onehot, axis=0, keepdims=True)
    cnt_ref[...] = carry_s[...].astype(jnp.int32)

    ri = jnp.where(lane == 0, i1, jnp.where(lane == 1, i2, jnp.where(lane == 2, rank1, rank2)))
    rg = jnp.where(lane == 0, g1, g2)
    ri_ref[...] = ri[:, :8]
    rg_ref[...] = rg[:, :8]


def _router(x2, gain, rw, tb):
    t, d = x2.shape
    return pl.pallas_call(
        _router_kernel,
        grid=(t // tb,),
        in_specs=[
            pl.BlockSpec((tb, d), lambda i: (i, 0)),
            pl.BlockSpec((1, d), lambda i: (0, 0)),
            pl.BlockSpec((d, LANES), lambda i: (0, 0)),
        ],
        out_specs=[
            pl.BlockSpec((tb * TILE_ROWS, LANES), lambda i: (i, 0)),
            pl.BlockSpec((tb, 8), lambda i: (i, 0)),
            pl.BlockSpec((tb, 8), lambda i: (i, 0)),
            pl.BlockSpec((1, LANES), lambda i: (0, 0)),
        ],
        out_shape=[
            jax.ShapeDtypeStruct((t * TILE_ROWS, LANES), F32),
            jax.ShapeDtypeStruct((t, 8), jnp.int32),
            jax.ShapeDtypeStruct((t, 8), F32),
            jax.ShapeDtypeStruct((1, LANES), jnp.int32),
        ],
        scratch_shapes=[pltpu.VMEM((1, LANES), F32)],
        compiler_params=_params(("arbitrary",)),
        name="router",
    )(x2, gain.reshape(1, d), rw)


def _tile_rows(i):
    return pl.ds(pl.multiple_of(i * TILE_ROWS, TILE_ROWS), TILE_ROWS)


def _dispatch_kernel(pad_lo_ref, pad_hi_ref, dest_ref, src_ref, dst_ref, sem):
    gb = dest_ref.shape[2] // 2

    def issue(t, carry):
        tile = src_ref.at[_tile_rows(t)]
        for k in range(2):
            pltpu.make_async_copy(tile, dst_ref.at[_tile_rows(dest_ref[0, 0, 2 * t + k])], sem).start(priority=k)
        return carry

    lax.fori_loop(0, gb, issue, 0, unroll=ISSUE_UNROLL)
    for _ in range(2):
        pltpu.make_async_copy(src_ref, dst_ref.at[pl.ds(0, gb * TILE_ROWS)], sem).wait()

    @pl.when(pl.program_id(0) == pl.num_programs(0) - 1)
    def _():
        for e in range(N_EXPERTS):
            lo, hi = pad_lo_ref[e], pad_hi_ref[e]

            def fill(r, carry):
                pltpu.make_async_copy(src_ref.at[_tile_rows(0)], dst_ref.at[_tile_rows(r)], sem).start()
                return carry

            def drain(r, carry):
                pltpu.make_async_copy(src_ref.at[_tile_rows(0)], dst_ref.at[_tile_rows(r)], sem).wait()
                return carry

            lax.fori_loop(lo, hi, fill, 0)
            lax.fori_loop(lo, hi, drain, 0)


def _dispatch(src, dest, pad_lo, pad_hi, n_rows, gb):
    t = src.shape[0] // TILE_ROWS
    return pl.pallas_call(
        _dispatch_kernel,
        grid_spec=pltpu.PrefetchScalarGridSpec(
            num_scalar_prefetch=2,
            grid=(t // gb,),
            in_specs=[
                pl.BlockSpec((1, 1, 2 * gb), lambda i, lo, hi: (i, 0, 0), memory_space=pltpu.SMEM),
                pl.BlockSpec((gb * TILE_ROWS, LANES), lambda i, lo, hi: (i, 0)),
            ],
            out_specs=pl.BlockSpec(memory_space=pl.ANY),
            scratch_shapes=[pltpu.SemaphoreType.DMA(())],
        ),
        out_shape=jax.ShapeDtypeStruct((n_rows * TILE_ROWS, LANES), src.dtype),
        compiler_params=_params(("arbitrary",)),
        name="moe_dispatch",
    )(pad_lo, pad_hi, dest.reshape(t // gb, 1, 2 * gb), src)


def _moe_kernel(te_ref, nu_ref, xs_ref, wgu_ref, wd_ref, out_ref, *, ff_chunk):
    used = pl.program_id(0) < nu_ref[0]

    @pl.when(used)
    def _():
        tm = xs_ref.shape[0] // TILE_ROWS
        h = jnp.concatenate([part.astype(BF16) for part in _load_token_tiles(xs_ref, tm)], axis=1)
        d_ff = wd_ref.shape[1]
        acc = None
        for c0 in range(0, d_ff, ff_chunk):
            c1 = min(c0 + ff_chunk, d_ff)
            gate = _dot(h, wgu_ref[0, :, c0:c1])
            up = _dot(h, wgu_ref[0, :, d_ff + c0:d_ff + c1])
            part = _dot((_silu(gate) * up).astype(BF16), wd_ref[0, c0:c1, :].astype(BF16))
            acc = part if acc is None else acc + part
        _store_token_tiles(out_ref, acc)

    @pl.when(jnp.logical_not(used))
    def _():
        out_ref[...] = jnp.zeros_like(out_ref)


def _moe(xs, tile_expert, n_used, w_gu, w_down, tm):
    n_rows = xs.shape[0] // TILE_ROWS
    d = w_down.shape[2]
    d_ff = w_down.shape[1]
    nt = n_rows // tm
    resident = pl.Buffered(1)
    return pl.pallas_call(
        functools.partial(_moe_kernel, ff_chunk=7 * MXU_WIDTH),
        grid_spec=pltpu.PrefetchScalarGridSpec(
            num_scalar_prefetch=2,
            grid=(nt,),
            in_specs=[
                pl.BlockSpec((tm * TILE_ROWS, LANES), lambda i, te, nu: (jnp.minimum(i, nu[0] - 1), 0)),
                pl.BlockSpec((1, d, 2 * d_ff), lambda i, te, nu: (te[i], 0, 0), pipeline_mode=resident),
                pl.BlockSpec((1, d_ff, d), lambda i, te, nu: (te[i], 0, 0), pipeline_mode=resident),
            ],
            out_specs=pl.BlockSpec((tm * TILE_ROWS, LANES), lambda i, te, nu: (i, 0)),
        ),
        out_shape=jax.ShapeDtypeStruct((n_rows * TILE_ROWS, LANES), F32),
        compiler_params=_params(("arbitrary",)),
        name="moe_experts",
    )(tile_expert, n_used, xs, w_gu, w_down)


def _combine_kernel(dest_ref, next_ref, x_ref, rg_ref, gain_ref, ys_ref, out_ref, buf, sem):
    tb = x_ref.shape[0]
    i = pl.program_id(0)
    n = pl.num_programs(0)
    slot = i % 2

    def gather(idx_ref, s):
        def issue(t, carry):
            for k in range(2):
                pltpu.make_async_copy(ys_ref.at[_tile_rows(idx_ref[0, 0, 2 * t + k])],
                                      buf.at[s, k, _tile_rows(t)], sem.at[s]).start(priority=k)
            return carry

        lax.fori_loop(0, tb, issue, 0, unroll=ISSUE_UNROLL)

    @pl.when(i == 0)
    def _():
        gather(dest_ref, 0)

    @pl.when(i + 1 < n)
    def _():
        gather(next_ref, 1 - slot)

    for k in range(2):
        pltpu.make_async_copy(ys_ref.at[pl.ds(0, tb * TILE_ROWS)], buf.at[slot, k], sem.at[slot]).wait()

    g = rg_ref[...]
    g1, g2 = g[:, 0:1], g[:, 1:2]
    ya = _load_token_tiles(buf.at[slot, 0], tb)
    yb = _load_token_tiles(buf.at[slot, 1], tb)
    y = jnp.concatenate([g1 * a + g2 * b for a, b in zip(ya, yb)], axis=1)
    out_ref[...] = _rms(x_ref[...] + y, gain_ref[...])


def _combine(x2, ys, dest, rg, gain, tb):
    t, d = x2.shape
    n = t // tb
    dest3 = dest.reshape(n, 1, 2 * tb)
    return pl.pallas_call(
        _combine_kernel,
        grid=(n,),
        in_specs=[
            pl.BlockSpec((1, 1, 2 * tb), lambda i: (i, 0, 0), memory_space=pltpu.SMEM),
            pl.BlockSpec((1, 1, 2 * tb), lambda i: (jnp.minimum(i + 1, n - 1), 0, 0), memory_space=pltpu.SMEM),
            pl.BlockSpec((tb, d), lambda i: (i, 0)),
            pl.BlockSpec((tb, 8), lambda i: (i, 0)),
            pl.BlockSpec((1, d), lambda i: (0, 0)),
            pl.BlockSpec(memory_space=pl.ANY),
        ],
        out_specs=pl.BlockSpec((tb, d), lambda i: (i, 0)),
        out_shape=jax.ShapeDtypeStruct((t, d), F32),
        scratch_shapes=[pltpu.VMEM((2, 2, tb * TILE_ROWS, LANES), F32), pltpu.SemaphoreType.DMA((2,))],
        compiler_params=_params(("arbitrary",)),
        name="moe_combine",
    )(dest3, dest3, x2, rg, gain.reshape(1, d), ys)


def _pad_heads(w, heads, width, padded):
    lead = w.shape[:-1]
    w = w.reshape(lead + (heads, width))
    w = jnp.pad(w, [(0, 0)] * len(lead) + [(0, 0), (0, padded - width)])
    return w.reshape(lead + (heads * padded,))


def _layer_a_weights(w_in, gate_w, gate_b, out_norm, w_out):
    dqk = GLA_HEADS * GLA_DK
    dv = GLA_HEADS * GLA_DV
    o = 0
    wq = w_in[:, o:o + dqk]; o += dqk
    wk = w_in[:, o:o + dqk]; o += dqk
    wv = w_in[:, o:o + dv]; o += dv
    wl = w_in[:, o:o + GLA_LOWRANK]; o += GLA_LOWRANK
    wr = w_in[:, o:o + dv]; o += dv
    wm = w_in[:, o:o + MEM_WIDTH]
    w = jnp.concatenate([
        _pad_heads(wq, GLA_HEADS, GLA_DK, GLA_DK_PAD),
        _pad_heads(wk, GLA_HEADS, GLA_DK, GLA_DK_PAD),
        _pad_heads(wv, GLA_HEADS, GLA_DV, GLA_DV_PAD),
        _pad_heads(wr, GLA_HEADS, GLA_DV, GLA_DV_PAD),
        wm,
        jnp.pad(wl, ((0, 0), (0, LANES - GLA_LOWRANK))),
    ], axis=1).astype(BF16)
    gw = _pad_heads(gate_w, GLA_HEADS, GLA_DK, GLA_DK_PAD)
    gw = jnp.pad(gw, ((0, LANES - GLA_LOWRANK), (0, 0))).astype(BF16)
    gb = _pad_heads(gate_b, GLA_HEADS, GLA_DK, GLA_DK_PAD).reshape(1, -1)
    og = jnp.pad(out_norm, (0, GLA_DV_PAD - GLA_DV)).reshape(1, -1)
    wo_main = w_out[:dv].reshape(GLA_HEADS, GLA_DV, -1)
    wo_main = jnp.pad(wo_main, ((0, 0), (0, GLA_DV_PAD - GLA_DV), (0, 0))).reshape(GLA_HEADS * GLA_DV_PAD, -1)
    wo = jnp.concatenate([wo_main, w_out[dv:]], axis=0).astype(BF16)
    return w, gw, gb, og, wo


def kernel(x, mem, positions, mix_norm, w_in_a, gla_gate_w, gla_gate_b, gla_out_norm, w_in_b, swa_sinks,
           mem_norm, w_mem_kv, w_out, ffn_norm, ffn_w_gate_up, ffn_w_down, router_w, exp_w_gate_up,
           exp_w_down, final_norm):
    b, s, d = x.shape
    t = b * s
    x2 = x.reshape(t, d)
    assert d == TILE_ROWS * LANES
    tq = min(512, s)
    tm_moe = 512
    gb = min(1024, t)
    gb_dispatch = min(2048, t)

    km, vm, vmt = _memkv(mem, mem_norm, w_mem_kv)

    wa, gw, gbias, og, wo_a = _layer_a_weights(w_in_a[0], gla_gate_w[0], gla_gate_b[0], gla_out_norm[0], w_out[0])
    z = _proj(x2, mix_norm[0], wa, tq)
    x2 = _gla_mix(z, x2, gw, gbias, og, km[0], vm[0], wo_a, b, s, tq)
    x2 = _ffn(x2, ffn_norm[0], ffn_w_gate_up[0].astype(BF16), ffn_w_down[0].astype(BF16), tq)

    n_q = SWA_HEADS * HEAD_DIM
    n_kv = SWA_KV_HEADS * HEAD_DIM
    wb = jnp.concatenate([w_in_b[0][:, :n_q] * (HEAD_DIM ** -0.5), w_in_b[0][:, n_q:]], axis=1).astype(BF16)
    half = ROPE_DIM // 2
    inv_freq = jnp.power(jnp.float32(ROPE_THETA), -jnp.arange(half, dtype=F32) / half)
    invf = jnp.tile(inv_freq, LANES // half).reshape(1, LANES)
    pos = jnp.repeat(positions.astype(F32).reshape(t), half).reshape(t * half // LANES, LANES)
    z = _proj(x2, mix_norm[1], wb, tq, rope=(pos, invf, n_q + n_kv))
    x2 = _swa_mix(z, x2, swa_sinks[0], km[1], vmt[1], w_out[1].astype(BF16), b, s, tq)

    rw = jnp.pad(router_w[0], ((0, 0), (0, LANES - N_EXPERTS)))
    h, ri, rg, counts = _router(x2, ffn_norm[1], rw, tq)

    cnt = counts[0, :N_EXPERTS]
    padded = ((cnt + tm_moe - 1) // tm_moe) * tm_moe
    ends = jnp.cumsum(padded)
    offs = ends - padded
    n_tiles = 2 * t // tm_moe + N_EXPERTS
    n_used = (ends[-1] // tm_moe).astype(jnp.int32)
    tile_start = jnp.arange(n_tiles, dtype=jnp.int32) * tm_moe
    tile_expert = jnp.sum(tile_start[:, None] >= ends[None, :], axis=1).astype(jnp.int32)
    tile_expert = jnp.minimum(tile_expert, N_EXPERTS - 1)
    last_expert = tile_expert[jnp.maximum(n_used - 1, 0)]
    tile_expert = jnp.where(jnp.arange(n_tiles) < n_used, tile_expert, last_expert)
    experts = ri[:, 0:2]
    off_of = jnp.sum(jnp.where(experts[:, :, None] == jnp.arange(N_EXPERTS)[None, None, :],
                               offs[None, None, :], 0), axis=-1)
    dest = (off_of + ri[:, 2:4]).astype(jnp.int32)

    pad_hi = ends.at[N_EXPERTS - 1].set(n_tiles * tm_moe)
    xs = _dispatch(h, dest, (offs + cnt).astype(jnp.int32), pad_hi.astype(jnp.int32), n_tiles * tm_moe,
                   gb_dispatch)
    ys = _moe(xs, tile_expert, n_used.reshape(1), exp_w_gate_up[0].astype(BF16), exp_w_down[0], tm_moe)
    out = _combine(x2, ys, dest, rg, final_norm, gb)
    return out.reshape(b, s, d)
```
